```python
import math, functools
import jax, jax.numpy as jnp
from jax import lax
import numpy as np

D_MODEL = 1024
BATCH = 8
SEQ = 2048
DEPTH = 4
DEC_BATCH = 128
DEC_SEQ = 4
PAST_LEN = 2048
PAGE_SIZE = 128

F32 = jnp.float32
D_SSM = D_MODEL // 2
D_ATTN = D_MODEL - D_SSM
SSM_GROUP = 16
N_SSM_GROUPS = D_SSM // SSM_GROUP
SSM_STATE = 64
HEAD_DIM = 64
N_HEADS = D_ATTN // HEAD_DIM
N_KV_HEADS = 2
N_REP = N_HEADS // N_KV_HEADS
N_IDX_HEADS = 8
IDX_DIM = 64
IDX_SCALE = (IDX_DIM ** -0.5) * (N_IDX_HEADS ** -0.5)
TOPK_MAX = 256
Q_BLOCK = 128
N_BUCKETS = 32
MAX_DISTANCE = 128
D_FF = 2816
N_EXPERTS = 8
TOP_K_EXPERTS = 2
D_FF_EXPERT = 1408
N_DENSE = (DEPTH + 1) // 2
N_MOE = DEPTH // 2
D_PLE = 256
LN_EPS = 1e-5
ALPHA = (2 * DEPTH) ** 0.25
BETA = (8 * DEPTH) ** -0.25
IN_SPLITS = (D_SSM, N_HEADS * HEAD_DIM, N_KV_HEADS * HEAD_DIM, N_KV_HEADS * HEAD_DIM,
             N_IDX_HEADS * IDX_DIM, IDX_DIM, N_IDX_HEADS)
D_IN = sum(IN_SPLITS)

kernel_name = 's5_dsa_hymba_deepnorm_step'


def layer_norm(x, g, b):
    xf = x.astype(F32)
    mu = jnp.mean(xf, axis=-1, keepdims=True)
    var = jnp.mean(jnp.square(xf - mu), axis=-1, keepdims=True)
    return ((xf - mu) * lax.rsqrt(var + LN_EPS) * g.astype(F32) + b.astype(F32)).astype(x.dtype)


def split_columns(z):
    outs = []
    start = 0
    for width in IN_SPLITS:
        outs.append(z[..., start:start + width])
        start += width
    return outs


def t5_bucket(dist):
    max_exact = N_BUCKETS // 2
    n = jnp.maximum(dist, 0)
    nf = jnp.maximum(n, 1).astype(F32)
    large = max_exact + (jnp.log(nf / max_exact) / math.log(MAX_DISTANCE / max_exact)
                         * (N_BUCKETS - max_exact)).astype(jnp.int32)
    large = jnp.minimum(large, N_BUCKETS - 1)
    return jnp.where(n < max_exact, n, large)


def s5_mixer(u, h0_re, h0_im, a_re, a_im, log_dt, b_re, b_im, c_re, c_im, d_skip, w_glu):
    bsz, length, _ = u.shape
    uf = u.astype(F32).reshape(bsz, length, N_SSM_GROUPS, SSM_GROUP)
    lam = lax.complex(a_re.astype(F32), a_im.astype(F32))
    dt = jnp.exp(log_dt.astype(F32))[:, None]
    a_bar = jnp.exp(lam * dt)
    b_bar = ((a_bar - 1.0) / lam)[:, :, None] * lax.complex(b_re.astype(F32), b_im.astype(F32))
    bu = jnp.einsum('blgc,gnc->blgn', uf.astype(jnp.complex64), b_bar)
    h0 = lax.complex(h0_re.astype(F32), h0_im.astype(F32))
    bu = bu.at[:, 0].add(a_bar * h0)
    a_seq = jnp.broadcast_to(a_bar, bu.shape)

    def combine(left, right):
        a_l, b_l = left
        a_r, b_r = right
        return a_r * a_l, a_r * b_l + b_r

    _, h = lax.associative_scan(combine, (a_seq, bu), axis=1)
    c = lax.complex(c_re.astype(F32), c_im.astype(F32))
    y = jnp.einsum('blgn,gcn->blgc', h, c).real + d_skip.astype(F32).reshape(N_SSM_GROUPS, SSM_GROUP) * uf
    y = jax.nn.gelu(y.reshape(bsz, length, D_SSM))
    z = y @ w_glu.astype(F32)
    out = z[..., :D_SSM] * jax.nn.sigmoid(z[..., D_SSM:])
    h_last = h[:, -1]
    return out.astype(u.dtype), h_last.real, h_last.imag


def indexer_scores(q_idx, w_idx, k_idx):
    s = jnp.einsum('bthd,bsd->bths', q_idx.astype(F32), k_idx.astype(F32))
    return jnp.einsum('bths,bth->bts', jax.nn.relu(s), w_idx.astype(F32)) * IDX_SCALE


def select_keys(scores, q_pos, topk):
    k_pos = jnp.arange(scores.shape[-1])
    causal = k_pos[None, None, :] <= q_pos[None, :, None]
    _, idx = lax.top_k(jnp.where(causal, scores, -jnp.inf), topk)
    valid = idx <= q_pos[None, :, None]
    return idx, valid


def sparse_attend(q, k_sel, v_sel, q_pos, idx, valid, rel_bias):
    bsz, t = q.shape[:2]
    qg = q.reshape(bsz, t, N_KV_HEADS, N_REP, HEAD_DIM)
    logits = jnp.einsum('btgrd,btkgd->btgrk', qg, k_sel).astype(F32) * (HEAD_DIM ** -0.5)
    bias = rel_bias[t5_bucket(q_pos[None, :, None] - idx)].astype(F32)
    bias = bias.reshape(bsz, t, -1, N_KV_HEADS, N_REP).transpose(0, 1, 3, 4, 2)
    logits = jnp.where(valid[:, :, None, None, :], logits + bias, -jnp.inf)
    probs = jax.nn.softmax(logits, axis=-1).astype(v_sel.dtype)
    out = jnp.einsum('btgrk,btkgd->btgrd', probs, v_sel)
    return out.reshape(bsz, t, N_HEADS * HEAD_DIM)


def gather_rows(rows, ids):
    return jax.vmap(lambda r, i: r[i])(rows, ids)


def prompt_attention(q, k, v, q_idx, w_idx, k_idx, rel_bias):
    bsz, seq = q.shape[:2]
    topk = min(TOPK_MAX, seq // 4)
    n_blocks = seq // Q_BLOCK

    def blockify(a):
        return jnp.moveaxis(a.reshape((bsz, n_blocks, Q_BLOCK) + a.shape[2:]), 1, 0)

    def one_block(args):
        qb, qib, wb, t0 = args
        q_pos = t0 + jnp.arange(Q_BLOCK)
        idx, valid = select_keys(indexer_scores(qib, wb, k_idx), q_pos, topk)
        return sparse_attend(qb, gather_rows(k, idx), gather_rows(v, idx), q_pos, idx, valid, rel_bias)

    out = lax.map(one_block, (blockify(q), blockify(q_idx), blockify(w_idx),
                              jnp.arange(n_blocks, dtype=jnp.int32) * Q_BLOCK))
    return jnp.moveaxis(out, 0, 1).reshape(bsz, seq, N_HEADS * HEAD_DIM)


def sample_attention(q, k_new, v_new, q_idx, w_idx, kidx_new, cache_k, cache_v, cache_kidx, page_table, rel_bias):
    bsz, t = q.shape[:2]
    n_pages = page_table.shape[1]
    past = n_pages * PAGE_SIZE
    topk = min(TOPK_MAX, (past + t) // 4)
    kidx_past = cache_kidx[page_table].reshape(bsz, past, IDX_DIM)
    kidx_all = jnp.concatenate([kidx_past, kidx_new.astype(kidx_past.dtype)], axis=1)
    q_pos = past + jnp.arange(t)
    idx, valid = select_keys(indexer_scores(q_idx, w_idx, kidx_all), q_pos, topk)
    pi = jnp.minimum(idx, past - 1)
    phys = page_table[jnp.arange(bsz)[:, None, None], pi // PAGE_SIZE]
    off = pi % PAGE_SIZE
    ni = jnp.clip(idx - past, 0, t - 1)
    is_new = (idx >= past)[..., None, None]
    k_sel = jnp.where(is_new, gather_rows(k_new, ni).astype(cache_k.dtype), cache_k[phys, off])
    v_sel = jnp.where(is_new, gather_rows(v_new, ni).astype(cache_v.dtype), cache_v[phys, off])
    return sparse_attend(q, k_sel.astype(q.dtype), v_sel.astype(q.dtype), q_pos, idx, valid, rel_bias)


def swiglu(x, w_gate, w_up, w_down):
    return (jax.nn.silu(x @ w_gate) * (x @ w_up)) @ w_down


def moe_swiglu(x, w_router, b_router, w_gate, w_up, w_down):
    logits = (x @ w_router).astype(F32) + b_router.astype(F32)
    top_vals, top_idx = lax.top_k(logits, TOP_K_EXPERTS)
    probs = jax.nn.softmax(top_vals, axis=-1)
    gates = jnp.sum(jax.nn.one_hot(top_idx, N_EXPERTS, dtype=F32) * probs[..., None], axis=-2)
    y = jnp.zeros_like(x)
    for e in range(N_EXPERTS):
        y = y + gates[..., e:e + 1].astype(x.dtype) * swiglu(x, w_gate[e], w_up[e], w_down[e])
    return y


def trunk_layer(x, p, attend, h0_re, h0_im, w_in, ssm_params, w_out, ln1_g, ln1_b,
                channel_mixer, ple_w_gate, ple_w_proj, ln2_g, ln2_b):
    bsz, t, _ = x.shape
    u, q, k, v, q_idx, k_idx, w_idx = split_columns(x @ w_in)
    q = q.reshape(bsz, t, N_HEADS, HEAD_DIM)
    k = k.reshape(bsz, t, N_KV_HEADS, HEAD_DIM)
    v = v.reshape(bsz, t, N_KV_HEADS, HEAD_DIM)
    q_idx = q_idx.reshape(bsz, t, N_IDX_HEADS, IDX_DIM)
    ssm_out, h_re, h_im = s5_mixer(u, h0_re, h0_im, *ssm_params)
    attn_out = attend(q, k, v, q_idx, w_idx, k_idx)
    mixed = jnp.concatenate([ssm_out, attn_out.astype(ssm_out.dtype)], axis=-1) @ w_out
    x = layer_norm(ALPHA * x + mixed, ln1_g, ln1_b)
    ple = jax.nn.sigmoid(x @ ple_w_gate) * (p @ ple_w_proj)
    x = layer_norm(ALPHA * x + channel_mixer(x) + ple, ln2_g, ln2_b)
    return x, k, v, k_idx, h_re, h_im


def setup_inputs(seed: int = 0) -> dict:
    key = jax.random.key(seed)
    keys = iter(jax.random.split(key, 64))

    def normal(shape, scale):
        return scale * jax.random.normal(next(keys), shape, F32)

    n_pages = PAST_LEN // PAGE_SIZE
    n_used = DEC_BATCH * n_pages
    n_pool = n_used + (n_used + 3) // 4
    perm = jax.random.permutation(next(keys), n_pool).astype(jnp.int32)
    page_table = perm[:n_used].reshape(DEC_BATCH, n_pages)

    a_im_base = jnp.pi * jnp.arange(SSM_STATE, dtype=F32)
    return {
        'x_prompt': normal((BATCH, SEQ, D_MODEL), 1.0),
        'x_sample': normal((DEC_BATCH, DEC_SEQ, D_MODEL), 1.0),
        'p_prompt': normal((DEPTH, BATCH, SEQ, D_PLE), 1.0),
        'p_sample': normal((DEPTH, DEC_BATCH, DEC_SEQ, D_PLE), 1.0),
        'cache_k': normal((DEPTH, n_pool, PAGE_SIZE, N_KV_HEADS, HEAD_DIM), 1.0),
        'cache_v': normal((DEPTH, n_pool, PAGE_SIZE, N_KV_HEADS, HEAD_DIM), 1.0),
        'cache_kidx': normal((DEPTH, n_pool, PAGE_SIZE, IDX_DIM), 1.0),
        'state_ssm_re': normal((DEPTH, DEC_BATCH, N_SSM_GROUPS, SSM_STATE), 0.1),
        'state_ssm_im': normal((DEPTH, DEC_BATCH, N_SSM_GROUPS, SSM_STATE), 0.1),
        'page_table': page_table,
        'ln_emb_g': 1.0 + normal((D_MODEL,), 0.01),
        'ln_emb_b': normal((D_MODEL,), 0.01),
        'w_in': normal((DEPTH, D_MODEL, D_IN), D_MODEL ** -0.5),
        'ssm_a_re': -0.5 + normal((DEPTH, N_SSM_GROUPS, SSM_STATE), 0.01),
        'ssm_a_im': a_im_base + normal((DEPTH, N_SSM_GROUPS, SSM_STATE), 0.01),
        'ssm_log_dt': jax.random.uniform(next(keys), (DEPTH, N_SSM_GROUPS), F32, math.log(1e-3), math.log(1e-1)),
        'ssm_b_re': normal((DEPTH, N_SSM_GROUPS, SSM_STATE, SSM_GROUP), (2 * SSM_GROUP) ** -0.5),
        'ssm_b_im': normal((DEPTH, N_SSM_GROUPS, SSM_STATE, SSM_GROUP), (2 * SSM_GROUP) ** -0.5),
        'ssm_c_re': normal((DEPTH, N_SSM_GROUPS, SSM_GROUP, SSM_STATE), SSM_STATE ** -0.5),
        'ssm_c_im': normal((DEPTH, N_SSM_GROUPS, SSM_GROUP, SSM_STATE), SSM_STATE ** -0.5),
        'ssm_d': normal((DEPTH, D_SSM), 1.0),
        'ssm_w_glu': normal((DEPTH, D_SSM, 2 * D_SSM), D_SSM ** -0.5),
        'rel_bias': normal((N_BUCKETS, N_HEADS), 0.3),
        'w_out': normal((DEPTH, D_MODEL, D_MODEL), BETA * D_MODEL ** -0.5),
        'ln1_g': 1.0 + normal((DEPTH, D_MODEL), 0.01),
        'ln1_b': normal((DEPTH, D_MODEL), 0.01),
        'ffn_w_gate': normal((N_DENSE, D_MODEL, D_FF), D_MODEL ** -0.5),
        'ffn_w_up': normal((N_DENSE, D_MODEL, D_FF), D_MODEL ** -0.5),
        'ffn_w_down': normal((N_DENSE, D_FF, D_MODEL), BETA * D_FF ** -0.5),
        'moe_w_router': normal((N_MOE, D_MODEL, N_EXPERTS), D_MODEL ** -0.5),
        'moe_b_router': normal((N_MOE, N_EXPERTS), 0.01),
        'moe_w_gate': normal((N_MOE, N_EXPERTS, D_MODEL, D_FF_EXPERT), D_MODEL ** -0.5),
        'moe_w_up': normal((N_MOE, N_EXPERTS, D_MODEL, D_FF_EXPERT), D_MODEL ** -0.5),
        'moe_w_down': normal((N_MOE, N_EXPERTS, D_FF_EXPERT, D_MODEL), BETA * D_FF_EXPERT ** -0.5),
        'ple_w_gate': normal((DEPTH, D_MODEL, D_MODEL), D_MODEL ** -0.5),
        'ple_w_proj': normal((DEPTH, D_PLE, D_MODEL), BETA * D_PLE ** -0.5),
        'ln2_g': 1.0 + normal((DEPTH, D_MODEL), 0.01),
        'ln2_b': normal((DEPTH, D_MODEL), 0.01),
    }


def reference(x_prompt, x_sample, p_prompt, p_sample, cache_k, cache_v, cache_kidx, state_ssm_re, state_ssm_im,
              page_table, ln_emb_g, ln_emb_b, w_in, ssm_a_re, ssm_a_im, ssm_log_dt, ssm_b_re, ssm_b_im,
              ssm_c_re, ssm_c_im, ssm_d, ssm_w_glu, rel_bias, w_out, ln1_g, ln1_b, ffn_w_gate, ffn_w_up,
              ffn_w_down, moe_w_router, moe_b_router, moe_w_gate, moe_w_up, moe_w_down, ple_w_gate,
              ple_w_proj, ln2_g, ln2_b):
    xp = layer_norm(x_prompt, ln_emb_g, ln_emb_b)
    xs = layer_norm(x_sample, ln_emb_g, ln_emb_b)
    h0_prompt = jnp.zeros((x_prompt.shape[0], N_SSM_GROUPS, SSM_STATE), F32)
    kp_l, vp_l, kip_l, hrp_l, hip_l = [], [], [], [], []
    ks_l, vs_l, kis_l, hrs_l, his_l = [], [], [], [], []
    prompt_attend = functools.partial(prompt_attention, rel_bias=rel_bias)
    for i in range(DEPTH):
        ssm_params = (ssm_a_re[i], ssm_a_im[i], ssm_log_dt[i], ssm_b_re[i], ssm_b_im[i],
                      ssm_c_re[i], ssm_c_im[i], ssm_d[i], ssm_w_glu[i])
        j = i // 2
        if i % 2 == 0:
            cm = functools.partial(swiglu, w_gate=ffn_w_gate[j], w_up=ffn_w_up[j], w_down=ffn_w_down[j])
        else:
            cm = functools.partial(moe_swiglu, w_router=moe_w_router[j], b_router=moe_b_router[j],
                                   w_gate=moe_w_gate[j], w_up=moe_w_up[j], w_down=moe_w_down[j])
        sample_attend = functools.partial(sample_attention, cache_k=cache_k[i], cache_v=cache_v[i],
                                          cache_kidx=cache_kidx[i], page_table=page_table, rel_bias=rel_bias)
        xp, kp, vp, kip, hrp, hip = trunk_layer(xp, p_prompt[i], prompt_attend, h0_prompt, h0_prompt, w_in[i],
                                                ssm_params, w_out[i], ln1_g[i], ln1_b[i], cm,
                                                ple_w_gate[i], ple_w_proj[i], ln2_g[i], ln2_b[i])
        xs, ks, vs, kis, hrs, his = trunk_layer(xs, p_sample[i], sample_attend, state_ssm_re[i], state_ssm_im[i],
                                                w_in[i], ssm_params, w_out[i], ln1_g[i], ln1_b[i], cm,
                                                ple_w_gate[i], ple_w_proj[i], ln2_g[i], ln2_b[i])
        kp_l.append(kp); vp_l.append(vp); kip_l.append(kip); hrp_l.append(hrp); hip_l.append(hip)
        ks_l.append(ks); vs_l.append(vs); kis_l.append(kis); hrs_l.append(hrs); his_l.append(his)
    return (xp, xs,
            jnp.stack(kp_l), jnp.stack(vp_l), jnp.stack(kip_l), jnp.stack(hrp_l), jnp.stack(hip_l),
            jnp.stack(ks_l), jnp.stack(vs_l), jnp.stack(kis_l), jnp.stack(hrs_l), jnp.stack(his_l))
```

```python
import functools
import math

import numpy as np
import jax
import jax.numpy as jnp
from jax import lax
from jax.experimental import pallas as pl
from jax.experimental.pallas import tpu as pltpu

F32 = jnp.float32
BF16 = jnp.bfloat16
I32 = jnp.int32

D_MODEL = 1024
DEPTH = 4
PAGE_SIZE = 128
D_SSM = 512
D_ATTN = 512
SSM_GROUP = 16
N_SSM_GROUPS = 32
SSM_STATE = 64
HEAD_DIM = 64
N_HEADS = 8
N_KV_HEADS = 2
N_REP = 4
N_IDX_HEADS = 8
IDX_DIM = 64
IDX_SCALE = (IDX_DIM ** -0.5) * (N_IDX_HEADS ** -0.5)
TOPK_MAX = 256
N_BUCKETS = 32
MAX_DISTANCE = 128
D_FF = 2816
N_EXPERTS = 8
D_FF_EXPERT = 1408
D_PLE = 256
LN_EPS = 1e-5
ALPHA = (2 * DEPTH) ** 0.25
D_IN = 1864
D_IN_PAD = 1920
N_STATE = N_SSM_GROUPS * SSM_STATE
N_SCHUNK = 4
SCHUNK = N_STATE // N_SCHUNK
UCHUNK = D_SSM // N_SCHUNK

INT_MIN = -2 ** 31
NEG_BIG = -1e30
VMEM_LIMIT = 56 * 1024 * 1024


def _cparams(sem):
    return pltpu.CompilerParams(dimension_semantics=sem, vmem_limit_bytes=VMEM_LIMIT)


def _vmem_spec():
    return pl.BlockSpec(memory_space=pltpu.VMEM)


def _layer_norm(x, g, b):
    mu = jnp.mean(x, axis=-1, keepdims=True)
    xc = x - mu
    var = jnp.mean(xc * xc, axis=-1, keepdims=True)
    return xc * lax.rsqrt(var + LN_EPS) * g + b


def _dot(a, b):
    return jnp.dot(a, b, preferred_element_type=F32)


def _ln_kernel(x_ref, g_ref, b_ref, o_ref):
    o_ref[...] = _layer_norm(x_ref[...], g_ref[...], b_ref[...])


def _ln_call(x, g, b, tm):
    m = x.shape[0]
    return pl.pallas_call(
        _ln_kernel,
        grid=(m // tm,),
        in_specs=[pl.BlockSpec((tm, D_MODEL), lambda i: (i, 0)), _vmem_spec(), _vmem_spec()],
        out_specs=pl.BlockSpec((tm, D_MODEL), lambda i: (i, 0)),
        out_shape=jax.ShapeDtypeStruct((m, D_MODEL), F32),
        compiler_params=_cparams(("parallel",)),
        name="ln_embed",
    )(x, g, b)


def _inproj_kernel(x_ref, w_ref, u_ref, q_ref, k_ref, v_ref, qi_ref, tail_ref, kb_ref, vb_ref, kib_ref):
    xb = x_ref[...].astype(BF16)
    u_ref[...] = _dot(xb, w_ref[:, 0:512])
    q_ref[...] = (_dot(xb, w_ref[:, 512:1024]) * (HEAD_DIM ** -0.5)).astype(BF16)
    k = _dot(xb, w_ref[:, 1024:1152])
    v = _dot(xb, w_ref[:, 1152:1280])
    k_ref[...] = k
    v_ref[...] = v
    kb_ref[...] = k.astype(BF16)
    vb_ref[...] = v.astype(BF16)
    qi_ref[...] = _dot(xb, w_ref[:, 1280:1792]).astype(BF16)
    tail = _dot(xb, w_ref[:, 1792:1920])
    tail_ref[...] = tail
    kib_ref[...] = tail[:, 0:IDX_DIM].astype(BF16)


def _inproj_call(x, w_pad, tm, u_tmajor_tiles):
    m = x.shape[0]
    row = lambda i: (i, 0)
    if u_tmajor_tiles:
        n = u_tmajor_tiles
        u_shape = (n * tm, (m // (n * tm)) * D_SSM)
        u_spec = pl.BlockSpec((tm, D_SSM), lambda i: (i % n, i // n))
    else:
        u_shape = (m, D_SSM)
        u_spec = pl.BlockSpec((tm, D_SSM), row)
    out_shape = (
        jax.ShapeDtypeStruct(u_shape, F32),
        jax.ShapeDtypeStruct((m, 512), BF16),
        jax.ShapeDtypeStruct((m, 128), F32),
        jax.ShapeDtypeStruct((m, 128), F32),
        jax.ShapeDtypeStruct((m, 512), BF16),
        jax.ShapeDtypeStruct((m, 128), F32),
        jax.ShapeDtypeStruct((m, 128), BF16),
        jax.ShapeDtypeStruct((m, 128), BF16),
        jax.ShapeDtypeStruct((m, IDX_DIM), BF16),
    )
    out_specs = (
        u_spec,
        pl.BlockSpec((tm, 512), row), pl.BlockSpec((tm, 128), row), pl.BlockSpec((tm, 128), row),
        pl.BlockSpec((tm, 512), row), pl.BlockSpec((tm, 128), row), pl.BlockSpec((tm, 128), row),
        pl.BlockSpec((tm, 128), row), pl.BlockSpec((tm, IDX_DIM), row),
    )
    return pl.pallas_call(
        _inproj_kernel,
        grid=(m // tm,),
        in_specs=[pl.BlockSpec((tm, D_MODEL), row), _vmem_spec()],
        out_specs=out_specs,
        out_shape=out_shape,
        compiler_params=_cparams(("parallel",)),
        name="in_proj",
    )(x, w_pad)


def _s5_kernel(u_ref, h0re_ref, h0im_ref, are_ref, aim_ref, bblk_ref, cblk_ref, d_ref, wglu_ref,
               out_ref, hre_ref, him_ref, hs_ref, st_ref, y_ref, *, rows_per_step, steps):
    c = pl.program_id(0)
    rps = rows_per_step
    n_rows = rps * steps

    @pl.when(c == 0)
    def _():
        st_ref[0] = h0re_ref[...]
        st_ref[1] = h0im_ref[...]

    ub = u_ref[...].astype(BF16)
    for j in range(N_SCHUNK):
        hs_ref[j] = _dot(ub[:, UCHUNK * j:UCHUNK * (j + 1)], bblk_ref[j])

    for j in range(N_SCHUNK):
        cols = slice(SCHUNK * j, SCHUNK * (j + 1))
        a_re = jnp.broadcast_to(are_ref[:, cols], (8, SCHUNK))
        a_im = jnp.broadcast_to(aim_ref[:, cols], (8, SCHUNK))
        for rg in range(rps // 8):
            rsl = slice(8 * rg, 8 * (rg + 1))

            def step(t, carry, j=j, rg=rg, a_re=a_re, a_im=a_im):
                h_re, h_im = carry
                row = pl.multiple_of(t * rps + 8 * rg, 8)
                n_re = a_re * h_re - a_im * h_im + hs_ref[j, pl.ds(row, 8), 0:SCHUNK]
                n_im = a_re * h_im + a_im * h_re + hs_ref[j, pl.ds(row, 8), SCHUNK:2 * SCHUNK]
                hs_ref[j, pl.ds(row, 8), 0:SCHUNK] = n_re
                hs_ref[j, pl.ds(row, 8), SCHUNK:2 * SCHUNK] = n_im
                return n_re, n_im

            h_re, h_im = lax.fori_loop(0, steps, step, (st_ref[0, rsl, cols], st_ref[1, rsl, cols]),
                                       unroll=min(steps, 8))
            st_ref[0, rsl, cols] = h_re
            st_ref[1, rsl, cols] = h_im

    rc = 256
    for r0 in range(0, n_rows, rc):
        for j in range(N_SCHUNK):
            y_ref[r0:r0 + rc, UCHUNK * j:UCHUNK * (j + 1)] = _dot(
                hs_ref[j, r0:r0 + rc, :].astype(BF16), cblk_ref[j])
    y = y_ref[...] + d_ref[...] * u_ref[...]
    g = jax.nn.gelu(y).astype(BF16)
    z = _dot(g, wglu_ref[...])
    out_ref[...] = (z[:, :D_SSM] * jax.nn.sigmoid(z[:, D_SSM:])).astype(out_ref.dtype)

    @pl.when(c == pl.num_programs(0) - 1)
    def _():
        hre_ref[...] = st_ref[0]
        him_ref[...] = st_ref[1]


def _s5_call(u_t, h0_re, h0_im, a_re, a_im, bblk, cblk, d, wglu, rows_per_step, steps):
    n_rows_total = u_t.shape[0]
    n_rows = rows_per_step * steps
    kern = functools.partial(_s5_kernel, rows_per_step=rows_per_step, steps=steps)
    return pl.pallas_call(
        kern,
        grid=(n_rows_total // n_rows,),
        in_specs=[pl.BlockSpec((n_rows, D_SSM), lambda c: (c, 0))] + [_vmem_spec()] * 8,
        out_specs=(pl.BlockSpec((n_rows, D_SSM), lambda c: (c, 0)),
                   pl.BlockSpec((rows_per_step, N_STATE), lambda c: (0, 0)),
                   pl.BlockSpec((rows_per_step, N_STATE), lambda c: (0, 0))),
        out_shape=(jax.ShapeDtypeStruct((n_rows_total, D_SSM), BF16),
                   jax.ShapeDtypeStruct((rows_per_step, N_STATE), F32),
                   jax.ShapeDtypeStruct((rows_per_step, N_STATE), F32)),
        scratch_shapes=[pltpu.VMEM((N_SCHUNK, n_rows, 2 * SCHUNK), F32),
                        pltpu.VMEM((2, rows_per_step, N_STATE), F32),
                        pltpu.VMEM((n_rows, D_SSM), F32)],
        compiler_params=_cparams(("arbitrary",)),
        name="s5_mixer",
    )(u_t, h0_re, h0_im, a_re, a_im, bblk, cblk, d, wglu)


def _sortable_key(score):
    bits = pltpu.bitcast(score + 0.0, I32)
    return jnp.where(bits < 0, bits ^ 0x7FFFFFFF, bits)


TQ = 256
TK = 256


def _attn_prompt_kernel(qiT_ref, wT_ref, kib_ref, qT_ref, k_ref, vT_ref, bias_ref, o_ref,
                        keys_ref, m_ref, l_ref, acc_ref, p_ref, *, topk):
    i = pl.program_id(1)
    n_kt = i + 1
    sub = lax.broadcasted_iota(I32, (TK, TQ), 0)
    lane = lax.broadcasted_iota(I32, (TK, TQ), 1)
    causal_diag = sub <= lane

    def score_body(j, carry):
        kt = kib_ref[pl.ds(pl.multiple_of(j * TK, TK), TK), :]
        acc = jnp.zeros((TK, TQ), F32)
        for h in range(N_IDX_HEADS):
            s = _dot(kt, qiT_ref[IDX_DIM * h:IDX_DIM * (h + 1), :])
            acc = acc + jnp.maximum(s, 0.0) * wT_ref[h:h + 1, :]
        key = _sortable_key(acc * IDX_SCALE)
        ok = jnp.logical_or(j < i, causal_diag)
        keys_ref[j] = jnp.where(ok, key, INT_MIN)
        return carry

    lax.fori_loop(0, n_kt, score_body, 0)

    def count(pred_fn):
        def body(j, c):
            hit = pred_fn(j, keys_ref[j]).astype(I32)
            return c + jnp.sum(hit.reshape(TK // 8, 8, TQ), axis=0)
        c8 = lax.fori_loop(0, n_kt, body, jnp.zeros((8, TQ), I32))
        return jnp.sum(c8, axis=0, keepdims=True)

    zero = jnp.zeros((1, TQ), I32)
    thr = jnp.where(count(lambda j, kk: kk >= zero) >= topk, zero, jnp.full((1, TQ), INT_MIN, I32))

    def bit_body(bi, thr):
        cand = thr | jnp.left_shift(jnp.int32(1), 30 - bi)
        return jnp.where(count(lambda j, kk: kk >= cand) >= topk, cand, thr)

    thr = lax.fori_loop(0, 31, bit_body, thr)

    cnt_ge = count(lambda j, kk: kk >= thr)
    p_ref[...] = jnp.full((1, TQ), 2 ** 30, I32)

    @pl.when(jnp.max(cnt_ge) > topk)
    def _():
        need = topk - count(lambda j, kk: kk > thr)

        def pos_body(bi, pos):
            cand = pos | jnp.left_shift(jnp.int32(1), 10 - bi)
            before = count(lambda j, kk: jnp.logical_and(kk == thr, sub + j * TK < cand))
            return jnp.where(before < need, cand, pos)

        p_ref[...] = lax.fori_loop(0, 11, pos_body, zero)

    pos = p_ref[...]
    m_ref[...] = jnp.full(m_ref.shape, NEG_BIG, F32)
    l_ref[...] = jnp.zeros(l_ref.shape, F32)
    acc_ref[...] = jnp.zeros(acc_ref.shape, F32)

    def attend_body(j, carry):
        kk = keys_ref[j]
        tie_ok = jnp.logical_and(kk == thr, sub + j * TK <= pos)
        sel = jnp.logical_and(jnp.logical_or(kk > thr, tie_ok), kk != INT_MIN)
        kind = jnp.minimum(i - j, 2)
        row0 = pl.multiple_of(j * TK, TK)
        for h in range(N_HEADS):
            g = h // N_REP
            hs = slice(HEAD_DIM * h, HEAD_DIM * (h + 1))
            s = _dot(k_ref[g, pl.ds(row0, TK), :], qT_ref[hs, :]) + bias_ref[h, kind]
            s = jnp.where(sel, s, NEG_BIG)
            m_old = m_ref[h]
            m_new = jnp.maximum(m_old, jnp.max(s, axis=0, keepdims=True))
            alpha = jnp.exp(m_old - m_new)
            p = jnp.where(sel, jnp.exp(s - m_new), 0.0)
            l_ref[h] = alpha * l_ref[h] + jnp.sum(p, axis=0, keepdims=True)
            pv = _dot(vT_ref[j, HEAD_DIM * g:HEAD_DIM * (g + 1), :], p.astype(BF16))
            acc_ref[hs, :] = alpha * acc_ref[hs, :] + pv
            m_ref[h] = m_new
        return carry

    lax.fori_loop(0, n_kt, attend_body, 0)

    for h in range(N_HEADS):
        hs = slice(HEAD_DIM * h, HEAD_DIM * (h + 1))
        acc_ref[hs, :] = acc_ref[hs, :] / l_ref[h]
    o_ref[...] = acc_ref[...].T.astype(o_ref.dtype)


def _attn_prompt_call(qiT, wT, kib, qT, k2, vT, biasT, bsz, seq, topk):
    nq = seq // TQ
    nkt = seq // TK
    kern = functools.partial(_attn_prompt_kernel, topk=topk)
    return pl.pallas_call(
        kern,
        grid=(bsz, nq),
        in_specs=[
            pl.BlockSpec((512, TQ), lambda b, i: (0, b * nq + i)),
            pl.BlockSpec((N_IDX_HEADS, TQ), lambda b, i: (0, b * nq + i)),
            pl.BlockSpec((seq, IDX_DIM), lambda b, i: (b, 0)),
            pl.BlockSpec((512, TQ), lambda b, i: (0, b * nq + i)),
            pl.BlockSpec((N_KV_HEADS, seq, HEAD_DIM), lambda b, i: (0, b, 0)),
            pl.BlockSpec((None, nkt, 128, TK), lambda b, i: (b, 0, 0, 0)),
            _vmem_spec(),
        ],
        out_specs=pl.BlockSpec((TQ, D_ATTN), lambda b, i: (b * nq + i, 0)),
        out_shape=jax.ShapeDtypeStruct((bsz * seq, D_ATTN), BF16),
        scratch_shapes=[pltpu.VMEM((nkt, TK, TQ), I32),
                        pltpu.VMEM((N_HEADS, 1, TQ), F32),
                        pltpu.VMEM((N_HEADS, 1, TQ), F32),
                        pltpu.VMEM((D_ATTN, TQ), F32),
                        pltpu.VMEM((1, TQ), I32)],
        compiler_params=_cparams(("parallel", "arbitrary")),
        name="attn_prompt",
    )(qiT, wT, kib, qT, k2, vT, biasT)


NB = 4
N_PAGES = 16
PAST = N_PAGES * PAGE_SIZE
S_ALL = PAST + PAGE_SIZE
T_DEC = 4
QROWS = T_DEC * N_HEADS


def _page_copies(pt_ref, ck_hbm, cv_hbm, cki_hbm, kbuf, vbuf, kibuf, sems, layer, step, slot):
    copies = []
    for n in range(NB):
        for pg in range(N_PAGES):
            page = pt_ref[step * NB + n, pg]
            dst = pl.ds(pg * PAGE_SIZE, PAGE_SIZE)
            copies.append(pltpu.make_async_copy(cki_hbm.at[layer, page], kibuf.at[slot, n, dst], sems.at[0, slot]))
            copies.append(pltpu.make_async_copy(ck_hbm.at[layer, page], kbuf.at[slot, n, dst], sems.at[1, slot]))
            copies.append(pltpu.make_async_copy(cv_hbm.at[layer, page], vbuf.at[slot, n, dst], sems.at[2, slot]))
    return copies


def _attn_sample_kernel(pt_ref, qi_ref, w_ref, q_ref, kin_ref, kn_ref, vn_ref, bias_ref,
                        ck_hbm, cv_hbm, cki_hbm, o_ref,
                        kbuf, vbuf, kibuf, keys_ref, prob_ref, sems, *, layer, topk):
    s = pl.program_id(0)
    n_steps = pl.num_programs(0)
    slot = lax.rem(s, 2)
    fetch = functools.partial(_page_copies, pt_ref, ck_hbm, cv_hbm, cki_hbm, kbuf, vbuf, kibuf, sems, layer)

    @pl.when(s == 0)
    def _():
        for cp in fetch(0, 0):
            cp.start()

    @pl.when(s + 1 < n_steps)
    def _():
        for cp in fetch(s + 1, 1 - slot):
            cp.start()

    for cp in fetch(s, slot):
        cp.wait()

    lane = lax.broadcasted_iota(I32, (1, S_ALL), 1)

    for n in range(NB):
        ki = jnp.concatenate([kibuf[slot, n].astype(BF16), kin_ref[n]], axis=0)
        sc = lax.dot_general(qi_ref[n], ki, (((1,), (1,)), ((), ())), preferred_element_type=F32)
        sc = jnp.maximum(sc, 0.0) * w_ref[n]
        for t in range(T_DEC):
            row = jnp.sum(sc[N_IDX_HEADS * t:N_IDX_HEADS * (t + 1), :], axis=0, keepdims=True) * IDX_SCALE
            key = jnp.where(lane <= PAST + t, _sortable_key(row), INT_MIN)
            keys_ref[T_DEC * n + t:T_DEC * n + t + 1, :] = key

    kk = keys_ref[...]
    nr = NB * T_DEC

    def count(hit):
        return jnp.sum(hit.astype(I32), axis=1, keepdims=True)

    zero = jnp.zeros((nr, 1), I32)
    thr = jnp.where(count(kk >= zero) >= topk, zero, jnp.full((nr, 1), INT_MIN, I32))

    def bit_body(bi, thr):
        cand = thr | jnp.left_shift(jnp.int32(1), 30 - bi)
        return jnp.where(count(kk >= cand) >= topk, cand, thr)

    thr = lax.fori_loop(0, 31, bit_body, thr)
    need = topk - count(kk > thr)
    eq = kk == thr
    lane_all = lax.broadcasted_iota(I32, (nr, S_ALL), 1)

    def pos_body(bi, pos):
        cand = pos | jnp.left_shift(jnp.int32(1), 11 - bi)
        before = count(jnp.logical_and(eq, lane_all < cand))
        return jnp.where(before < need, cand, pos)

    pos = lax.fori_loop(0, 12, pos_body, zero)
    sel_all = jnp.logical_and(jnp.logical_or(kk > thr, jnp.logical_and(eq, lane_all <= pos)), kk != INT_MIN)
    keys_ref[...] = sel_all.astype(I32)

    for n in range(NB):
        kcat = jnp.concatenate([kbuf[slot, n].astype(BF16), kn_ref[n]], axis=0)
        vcat = jnp.concatenate([vbuf[slot, n].astype(BF16), vn_ref[n]], axis=0)
        lg = lax.dot_general(q_ref[n], kcat, (((1,), (1,)), ((), ())), preferred_element_type=F32)
        lg = lg + bias_ref[...]
        for t in range(T_DEC):
            rs = slice(N_HEADS * t, N_HEADS * (t + 1))
            sel = keys_ref[T_DEC * n + t:T_DEC * n + t + 1, :] > 0
            x = jnp.where(sel, lg[rs, :], NEG_BIG)
            mx = jnp.max(x, axis=1, keepdims=True)
            e = jnp.where(sel, jnp.exp(x - mx), 0.0)
            prob_ref[rs, :] = e / jnp.sum(e, axis=1, keepdims=True)
        o_ref[n] = _dot(prob_ref[...].astype(BF16), vcat)


def _attn_sample_call(page_table, qi_s, w_s, q_s, kin, kn, vn, bias_s, cache_k2, cache_v2, cache_kidx, layer, topk):
    bsz = qi_s.shape[0]
    kern = functools.partial(_attn_sample_kernel, layer=layer, topk=topk)
    blk3 = lambda shp: pl.BlockSpec((NB,) + shp, lambda s, pt: (s, 0, 0))
    any_spec = pl.BlockSpec(memory_space=pl.ANY)
    grid_spec = pltpu.PrefetchScalarGridSpec(
        num_scalar_prefetch=1,
        grid=(bsz // NB,),
        in_specs=[blk3((QROWS, IDX_DIM)), blk3((QROWS, 1)), blk3((QROWS, 128)),
                  blk3((PAGE_SIZE, IDX_DIM)), blk3((PAGE_SIZE, 128)), blk3((PAGE_SIZE, 128)),
                  pl.BlockSpec((QROWS, S_ALL), lambda s, pt: (0, 0)),
                  any_spec, any_spec, any_spec],
        out_specs=blk3((QROWS, 128)),
        scratch_shapes=[pltpu.VMEM((2, NB, PAST, 128), F32),
                        pltpu.VMEM((2, NB, PAST, 128), F32),
                        pltpu.VMEM((2, NB, PAST, IDX_DIM), F32),
                        pltpu.VMEM((NB * T_DEC, S_ALL), I32),
                        pltpu.VMEM((QROWS, S_ALL), F32),
                        pltpu.SemaphoreType.DMA((3, 2))],
    )
    return pl.pallas_call(
        kern,
        grid_spec=grid_spec,
        out_shape=jax.ShapeDtypeStruct((bsz, QROWS, 128), F32),
        compiler_params=_cparams(("arbitrary",)),
        name="attn_sample",
    )(page_table, qi_s, w_s, q_s, kin, kn, vn, bias_s, cache_k2, cache_v2, cache_kidx)


def _outproj_kernel(x_ref, ssm_ref, attn_ref, wt_ref, wb_ref, g_ref, b_ref, o_ref):
    mixed = _dot(ssm_ref[...], wt_ref[...]) + _dot(attn_ref[...], wb_ref[...])
    o_ref[...] = _layer_norm(ALPHA * x_ref[...] + mixed, g_ref[...], b_ref[...])


def _outproj_call(x, ssm, attn, w_top, w_bot, g, b, tm, ssm_tmajor_tiles):
    m = x.shape[0]
    row = lambda i: (i, 0)
    if ssm_tmajor_tiles:
        n = ssm_tmajor_tiles
        ssm_spec = pl.BlockSpec((tm, D_SSM), lambda i: (i % n, i // n))
    else:
        ssm_spec = pl.BlockSpec((tm, D_SSM), row)
    return pl.pallas_call(
        _outproj_kernel,
        grid=(m // tm,),
        in_specs=[pl.BlockSpec((tm, D_MODEL), row), ssm_spec, pl.BlockSpec((tm, D_ATTN), row),
                  _vmem_spec(), _vmem_spec(), _vmem_spec(), _vmem_spec()],
        out_specs=pl.BlockSpec((tm, D_MODEL), row),
        out_shape=jax.ShapeDtypeStruct((m, D_MODEL), F32),
        compiler_params=_cparams(("parallel",)),
        name="out_proj_ln1",
    )(x, ssm, attn, w_top, w_bot, g, b)


def _ffn_kernel(x_ref, wg_ref, wu_ref, wd_ref, o_ref, xb_ref, acc_ref):
    f = pl.program_id(1)

    @pl.when(f == 0)
    def _():
        xb_ref[...] = x_ref[...].astype(BF16)
        acc_ref[...] = jnp.zeros(acc_ref.shape, F32)

    xb = xb_ref[...]
    h = jax.nn.silu(_dot(xb, wg_ref[...])) * _dot(xb, wu_ref[...])
    acc_ref[...] += _dot(h.astype(BF16), wd_ref[...])

    @pl.when(f == pl.num_programs(1) - 1)
    def _():
        o_ref[...] = acc_ref[...]


def _ffn_call(x, wg, wu, wd, tm, tf):
    m = x.shape[0]
    dff = wg.shape[1]
    return pl.pallas_call(
        _ffn_kernel,
        grid=(m // tm, dff // tf),
        in_specs=[pl.BlockSpec((tm, D_MODEL), lambda i, f: (i, 0)),
                  pl.BlockSpec((D_MODEL, tf), lambda i, f: (0, f)),
                  pl.BlockSpec((D_MODEL, tf), lambda i, f: (0, f)),
                  pl.BlockSpec((tf, D_MODEL), lambda i, f: (f, 0))],
        out_specs=pl.BlockSpec((tm, D_MODEL), lambda i, f: (i, 0)),
        out_shape=jax.ShapeDtypeStruct((m, D_MODEL), F32),
        scratch_shapes=[pltpu.VMEM((tm, D_MODEL), BF16), pltpu.VMEM((tm, D_MODEL), F32)],
        compiler_params=_cparams(("parallel", "arbitrary")),
        name="ffn_swiglu",
    )(x, wg, wu, wd)


def _router_kernel(x_ref, wr_ref, br_ref, g_ref):
    logits = jnp.dot(x_ref[...], wr_ref[...], preferred_element_type=F32,
                     precision=lax.Precision.HIGHEST) + br_ref[...]
    lane = lax.broadcasted_iota(I32, logits.shape, 1)
    logits = jnp.where(lane < N_EXPERTS, logits, -jnp.inf)
    lane_f = lane.astype(F32)
    m1 = jnp.max(logits, axis=1, keepdims=True)
    i1 = jnp.min(jnp.where(logits == m1, lane_f, 128.0), axis=1, keepdims=True)
    rest = jnp.where(lane_f == i1, -jnp.inf, logits)
    m2 = jnp.max(rest, axis=1, keepdims=True)
    i2 = jnp.min(jnp.where(rest == m2, lane_f, 128.0), axis=1, keepdims=True)
    e2 = jnp.exp(m2 - m1)
    den = 1.0 + e2
    g_ref[...] = jnp.where(lane_f == i1, 1.0 / den, 0.0) + jnp.where(lane_f == i2, e2 / den, 0.0)


def _router_call(x, wr_pad, br_pad, tm):
    m = x.shape[0]
    return pl.pallas_call(
        _router_kernel,
        grid=(m // tm,),
        in_specs=[pl.BlockSpec((tm, D_MODEL), lambda i: (i, 0)), _vmem_spec(), _vmem_spec()],
        out_specs=pl.BlockSpec((tm, 128), lambda i: (i, 0)),
        out_shape=jax.ShapeDtypeStruct((m, 128), F32),
        compiler_params=_cparams(("parallel",)),
        name="moe_router",
    )(x, wr_pad, br_pad)


def _moe_kernel(x_ref, gate_ref, wg_ref, wu_ref, wd_ref, o_ref, xb_ref, acc_ref):
    e = pl.program_id(1)

    @pl.when(e == 0)
    def _():
        xb_ref[...] = x_ref[...].astype(BF16)
        acc_ref[...] = jnp.zeros(acc_ref.shape, F32)

    xb = xb_ref[...]
    h = jax.nn.silu(_dot(xb, wg_ref[...])) * _dot(xb, wu_ref[...])
    y = _dot(h.astype(BF16), wd_ref[...])
    gates = gate_ref[...]
    lane = lax.broadcasted_iota(I32, gates.shape, 1)
    ge = jnp.sum(jnp.where(lane == e, gates, 0.0), axis=1, keepdims=True)
    acc_ref[...] += ge * y

    @pl.when(e == pl.num_programs(1) - 1)
    def _():
        o_ref[...] = acc_ref[...]


def _moe_call(x, gates, wg, wu, wd, tm):
    m = x.shape[0]
    return pl.pallas_call(
        _moe_kernel,
        grid=(m // tm, N_EXPERTS),
        in_specs=[pl.BlockSpec((tm, D_MODEL), lambda i, e: (i, 0)),
                  pl.BlockSpec((tm, 128), lambda i, e: (i, 0)),
                  pl.BlockSpec((None, D_MODEL, D_FF_EXPERT), lambda i, e: (e, 0, 0)),
                  pl.BlockSpec((None, D_MODEL, D_FF_EXPERT), lambda i, e: (e, 0, 0)),
                  pl.BlockSpec((None, D_FF_EXPERT, D_MODEL), lambda i, e: (e, 0, 0))],
        out_specs=pl.BlockSpec((tm, D_MODEL), lambda i, e: (i, 0)),
        out_shape=jax.ShapeDtypeStruct((m, D_MODEL), F32),
        scratch_shapes=[pltpu.VMEM((tm, D_MODEL), BF16), pltpu.VMEM((tm, D_MODEL), F32)],
        compiler_params=_cparams(("parallel", "arbitrary")),
        name="moe_swiglu",
    )(x, gates, wg, wu, wd)


def _ple_ln2_kernel(x_ref, cm_ref, p_ref, wpg_ref, wpp_ref, g_ref, b_ref, o_ref):
    x = x_ref[...]
    gate = jax.nn.sigmoid(_dot(x.astype(BF16), wpg_ref[...]))
    ple = gate * _dot(p_ref[...].astype(BF16), wpp_ref[...])
    o_ref[...] = _layer_norm(ALPHA * x + cm_ref[...] + ple, g_ref[...], b_ref[...])


def _ple_ln2_call(x, cm, p, wpg, wpp, g, b, tm):
    m = x.shape[0]
    row = lambda i: (i, 0)
    return pl.pallas_call(
        _ple_ln2_kernel,
        grid=(m // tm,),
        in_specs=[pl.BlockSpec((tm, D_MODEL), row), pl.BlockSpec((tm, D_MODEL), row),
                  pl.BlockSpec((tm, D_PLE), row), _vmem_spec(), _vmem_spec(), _vmem_spec(), _vmem_spec()],
        out_specs=pl.BlockSpec((tm, D_MODEL), row),
        out_shape=jax.ShapeDtypeStruct((m, D_MODEL), F32),
        compiler_params=_cparams(("parallel",)),
        name="ple_ln2",
    )(x, cm, p, wpg, wpp, g, b)


def _t5_bucket(dist):
    max_exact = N_BUCKETS // 2
    n = jnp.maximum(dist, 0)
    nf = jnp.maximum(n, 1).astype(F32)
    large = max_exact + (jnp.log(nf / max_exact) / math.log(MAX_DISTANCE / max_exact)
                         * (N_BUCKETS - max_exact)).astype(I32)
    large = jnp.minimum(large, N_BUCKETS - 1)
    return jnp.where(n < max_exact, n, large)


def _ssm_matrices(a_re, a_im, log_dt, b_re, b_im, c_re, c_im):
    lam = lax.complex(a_re, a_im)
    dt = jnp.exp(log_dt)[:, None]
    a_bar = jnp.exp(lam * dt)
    b_bar = ((a_bar - 1.0) / lam)[:, :, None] * lax.complex(b_re, b_im)
    gpc = N_SSM_GROUPS // N_SCHUNK
    eye = jnp.eye(gpc, dtype=F32)

    def in_block(x):
        x = x.reshape(N_SCHUNK, gpc, SSM_STATE, SSM_GROUP)
        blk = jnp.einsum('jgnc,gh->jgchn', x, eye)
        return blk.reshape(N_SCHUNK, gpc * SSM_GROUP, gpc * SSM_STATE)

    def out_block(x):
        x = x.reshape(N_SCHUNK, gpc, SSM_GROUP, SSM_STATE)
        blk = jnp.einsum('jgcn,gh->jgnhc', x, eye)
        return blk.reshape(N_SCHUNK, gpc * SSM_STATE, gpc * SSM_GROUP)

    bblk = jnp.concatenate([in_block(b_bar.real), in_block(b_bar.imag)], axis=2).astype(BF16)
    cblk = jnp.concatenate([out_block(c_re), out_block(-c_im)], axis=1).astype(BF16)
    return (a_bar.real.reshape(1, N_STATE), a_bar.imag.reshape(1, N_STATE), bblk, cblk)


def _prompt_bias_tiles(rel_bias):
    c = np.arange(TK)[:, None]
    r = np.arange(TQ)[None, :]
    dist = np.stack([kind * TQ + r - c for kind in range(3)]).astype(np.int32)
    bias = rel_bias[_t5_bucket(jnp.asarray(dist))]
    return jnp.transpose(bias, (3, 0, 1, 2)).astype(F32)


def _sample_bias_rows(rel_bias):
    key_pos = np.arange(S_ALL)[None, :]
    q_pos = PAST + np.arange(T_DEC)[:, None]
    dist = (q_pos - key_pos).astype(np.int32)
    bias = rel_bias[_t5_bucket(jnp.asarray(dist))]
    return jnp.transpose(bias, (0, 2, 1)).reshape(QROWS, S_ALL).astype(F32)


def kernel(x_prompt, x_sample, p_prompt, p_sample, cache_k, cache_v, cache_kidx, state_ssm_re, state_ssm_im, page_table, ln_emb_g, ln_emb_b, w_in, ssm_a_re, ssm_a_im, ssm_log_dt, ssm_b_re, ssm_b_im, ssm_c_re, ssm_c_im, ssm_d, ssm_w_glu, rel_bias, w_out, ln1_g, ln1_b, ffn_w_gate, ffn_w_up, ffn_w_down, moe_w_router, moe_b_router, moe_w_gate, moe_w_up, moe_w_down, ple_w_gate, ple_w_proj, ln2_g, ln2_b):
    bsz, seq, _ = x_prompt.shape
    dbsz, dseq, _ = x_sample.shape
    assert dseq == T_DEC and page_table.shape[1] == N_PAGES and dbsz % NB == 0
    mp = bsz * seq
    ms = dbsz * dseq
    tm = 512
    tiles_per_seq = seq // tm
    topk_p = min(TOPK_MAX, seq // 4)
    topk_s = min(TOPK_MAX, (PAST + dseq) // 4)
    n_pool = cache_k.shape[1]

    row2 = lambda a: a.reshape(1, -1)
    xp = _ln_call(x_prompt.reshape(mp, D_MODEL), row2(ln_emb_g), row2(ln_emb_b), tm)
    xs = _ln_call(x_sample.reshape(ms, D_MODEL), row2(ln_emb_g), row2(ln_emb_b), tm)

    cache_k2 = cache_k.reshape(DEPTH, n_pool, PAGE_SIZE, 128)
    cache_v2 = cache_v.reshape(DEPTH, n_pool, PAGE_SIZE, 128)
    bias_p = _prompt_bias_tiles(rel_bias)
    bias_s = _sample_bias_rows(rel_bias)
    h0_prompt = jnp.zeros((bsz, N_STATE), F32)

    outs = [[] for _ in range(10)]
    for i in range(DEPTH):
        w_pad = jnp.pad(w_in[i], ((0, 0), (0, D_IN_PAD - D_IN))).astype(BF16)
        a_re, a_im, bblk, cblk = _ssm_matrices(ssm_a_re[i], ssm_a_im[i], ssm_log_dt[i], ssm_b_re[i], ssm_b_im[i],
                                               ssm_c_re[i], ssm_c_im[i])
        d_row = row2(ssm_d[i])
        wglu = ssm_w_glu[i].astype(BF16)
        w_top = w_out[i, :D_SSM].astype(BF16)
        w_bot = w_out[i, D_SSM:].astype(BF16)
        wpg = ple_w_gate[i].astype(BF16)
        wpp = ple_w_proj[i].astype(BF16)
        j = i // 2

        def channel_mixer(x1, tm_cm):
            if i % 2 == 0:
                return _ffn_call(x1, ffn_w_gate[j].astype(BF16), ffn_w_up[j].astype(BF16),
                                 ffn_w_down[j].astype(BF16), tm_cm, D_FF // 2)
            wr_pad = jnp.pad(moe_w_router[j], ((0, 0), (0, 128 - N_EXPERTS)))
            br_pad = jnp.pad(moe_b_router[j], (0, 128 - N_EXPERTS)).reshape(1, 128)
            gates = _router_call(x1, wr_pad, br_pad, 512)
            return _moe_call(x1, gates, moe_w_gate[j].astype(BF16), moe_w_up[j].astype(BF16),
                             moe_w_down[j].astype(BF16), tm_cm)

        u_t, q, k, v, qi, tail, kb, vb, kib = _inproj_call(xp, w_pad, tm, tiles_per_seq)
        ssm_t, hre, him = _s5_call(u_t.reshape(seq * bsz, D_SSM), h0_prompt, h0_prompt, a_re, a_im, bblk, cblk,
                                   d_row, wglu, rows_per_step=bsz, steps=128)
        k2 = jnp.transpose(kb.reshape(mp, N_KV_HEADS, HEAD_DIM), (1, 0, 2))
        vT = jnp.transpose(vb.reshape(bsz, seq // TK, TK, 128), (0, 1, 3, 2))
        attn = _attn_prompt_call(qi.T, tail[:, IDX_DIM:IDX_DIM + N_IDX_HEADS].T, kib, q.T, k2, vT, bias_p,
                                 bsz, seq, topk_p)
        x1 = _outproj_call(xp, ssm_t.reshape(seq, bsz * D_SSM), attn, w_top, w_bot, row2(ln1_g[i]), row2(ln1_b[i]),
                           tm, tiles_per_seq)
        cm = channel_mixer(x1, 512)
        xp = _ple_ln2_call(x1, cm, p_prompt[i].reshape(mp, D_PLE), wpg, wpp, row2(ln2_g[i]), row2(ln2_b[i]), tm)
        outs[0].append(k.reshape(bsz, seq, N_KV_HEADS, HEAD_DIM))
        outs[1].append(v.reshape(bsz, seq, N_KV_HEADS, HEAD_DIM))
        outs[2].append(tail[:, :IDX_DIM].reshape(bsz, seq, IDX_DIM))
        outs[3].append(hre.reshape(bsz, N_SSM_GROUPS, SSM_STATE))
        outs[4].append(him.reshape(bsz, N_SSM_GROUPS, SSM_STATE))

        u, q, k, v, qi, tail, kb, vb, kib = _inproj_call(xs, w_pad, tm, 0)
        u_t = jnp.transpose(u.reshape(dbsz, dseq, D_SSM), (1, 0, 2)).reshape(ms, D_SSM)
        ssm_t, hre, him = _s5_call(u_t, state_ssm_re[i].reshape(dbsz, N_STATE), state_ssm_im[i].reshape(dbsz, N_STATE),
                                   a_re, a_im, bblk, cblk, d_row, wglu, rows_per_step=dbsz, steps=dseq)
        ssm = jnp.transpose(ssm_t.reshape(dseq, dbsz, D_SSM), (1, 0, 2)).reshape(ms, D_SSM)
        q4 = q.reshape(dbsz, dseq, N_KV_HEADS, N_REP, HEAD_DIM)
        q_exp = jnp.einsum('btgrd,gh->btgrhd', q4, jnp.eye(N_KV_HEADS, dtype=BF16)).reshape(dbsz, QROWS, 128)
        pad_new = lambda a: jnp.pad(a.reshape(dbsz, dseq, -1), ((0, 0), (0, PAGE_SIZE - dseq), (0, 0)))
        o_s = _attn_sample_call(page_table, qi.reshape(dbsz, QROWS, IDX_DIM),
                                tail[:, IDX_DIM:IDX_DIM + N_IDX_HEADS].reshape(dbsz, QROWS, 1), q_exp,
                                pad_new(kib), pad_new(kb), pad_new(vb), bias_s,
                                cache_k2, cache_v2, cache_kidx, i, topk_s)
        o5 = o_s.reshape(dbsz, dseq, N_KV_HEADS, N_REP, N_KV_HEADS, HEAD_DIM)
        attn = jnp.einsum('btgrhd,gh->btgrd', o5, jnp.eye(N_KV_HEADS, dtype=F32)).reshape(ms, D_ATTN).astype(BF16)
        x1 = _outproj_call(xs, ssm, attn, w_top, w_bot, row2(ln1_g[i]), row2(ln1_b[i]), tm, 0)
        cm = channel_mixer(x1, 512)
        xs = _ple_ln2_call(x1, cm, p_sample[i].reshape(ms, D_PLE), wpg, wpp, row2(ln2_g[i]), row2(ln2_b[i]), tm)
        outs[5].append(k.reshape(dbsz, dseq, N_KV_HEADS, HEAD_DIM))
        outs[6].append(v.reshape(dbsz, dseq, N_KV_HEADS, HEAD_DIM))
        outs[7].append(tail[:, :IDX_DIM].reshape(dbsz, dseq, IDX_DIM))
        outs[8].append(hre.reshape(dbsz, N_SSM_GROUPS, SSM_STATE))
        outs[9].append(him.reshape(dbsz, N_SSM_GROUPS, SSM_STATE))

    return (xp.reshape(bsz, seq, D_MODEL), xs.reshape(dbsz, dseq, D_MODEL)) + tuple(jnp.stack(o) for o in outs)
```

```python
import functools
import math

import numpy as np
import jax
import jax.numpy as jnp
from jax import lax
from jax.experimental import pallas as pl
from jax.experimental.pallas import tpu as pltpu

F32 = jnp.float32
BF16 = jnp.bfloat16
I32 = jnp.int32

D_MODEL = 1024
DEPTH = 4
PAGE_SIZE = 128
D_SSM = 512
D_ATTN = 512
SSM_GROUP = 16
N_SSM_GROUPS = 32
SSM_STATE = 64
HEAD_DIM = 64
N_HEADS = 8
N_KV_HEADS = 2
N_REP = 4
N_IDX_HEADS = 8
IDX_DIM = 64
IDX_SCALE = (IDX_DIM ** -0.5) * (N_IDX_HEADS ** -0.5)
TOPK_MAX = 256
N_BUCKETS = 32
MAX_DISTANCE = 128
D_FF = 2816
N_EXPERTS = 8
D_FF_EXPERT = 1408
D_PLE = 256
LN_EPS = 1e-5
ALPHA = (2 * DEPTH) ** 0.25
D_IN = 1864
D_IN_PAD = 1920
N_STATE = N_SSM_GROUPS * SSM_STATE
N_SCHUNK = 4
SCHUNK = N_STATE // N_SCHUNK
UCHUNK = D_SSM // N_SCHUNK

INT_MIN = -2 ** 31
NEG_BIG = -1e30
VMEM_LIMIT = 56 * 1024 * 1024


def _cparams(sem):
    return pltpu.CompilerParams(dimension_semantics=sem, vmem_limit_bytes=VMEM_LIMIT)


def _vmem_spec():
    return pl.BlockSpec(memory_space=pltpu.VMEM)


def _layer_norm(x, g, b):
    mu = jnp.mean(x, axis=-1, keepdims=True)
    xc = x - mu
    var = jnp.mean(xc * xc, axis=-1, keepdims=True)
    return xc * lax.rsqrt(var + LN_EPS) * g + b


def _dot(a, b):
    return jnp.dot(a, b, preferred_element_type=F32)


def _ln_kernel(x_ref, g_ref, b_ref, o_ref):
    o_ref[...] = _layer_norm(x_ref[...], g_ref[...], b_ref[...])


def _ln_call(x, g, b, tm):
    m = x.shape[0]
    return pl.pallas_call(
        _ln_kernel,
        grid=(m // tm,),
        in_specs=[pl.BlockSpec((tm, D_MODEL), lambda i: (i, 0)), _vmem_spec(), _vmem_spec()],
        out_specs=pl.BlockSpec((tm, D_MODEL), lambda i: (i, 0)),
        out_shape=jax.ShapeDtypeStruct((m, D_MODEL), F32),
        compiler_params=_cparams(("parallel",)),
        name="ln_embed",
    )(x, g, b)


def _inproj_kernel(x_ref, w_ref, u_ref, q_ref, k_ref, v_ref, qi_ref, tail_ref, kb_ref, vb_ref, kib_ref):
    xb = x_ref[...].astype(BF16)
    u_ref[...] = _dot(xb, w_ref[:, 0:512])
    q_ref[...] = (_dot(xb, w_ref[:, 512:1024]) * (HEAD_DIM ** -0.5)).astype(BF16)
    k = _dot(xb, w_ref[:, 1024:1152])
    v = _dot(xb, w_ref[:, 1152:1280])
    k_ref[...] = k
    v_ref[...] = v
    kb_ref[...] = k.astype(BF16)
    vb_ref[...] = v.astype(BF16)
    qi_ref[...] = _dot(xb, w_ref[:, 1280:1792]).astype(BF16)
    tail = _dot(xb, w_ref[:, 1792:1920])
    tail_ref[...] = tail
    kib_ref[...] = tail[:, 0:IDX_DIM].astype(BF16)


def _inproj_call(x, w_pad, tm, u_tmajor_tiles):
    m = x.shape[0]
    row = lambda i: (i, 0)
    if u_tmajor_tiles:
        n = u_tmajor_tiles
        u_shape = (n * tm, (m // (n * tm)) * D_SSM)
        u_spec = pl.BlockSpec((tm, D_SSM), lambda i: (i % n, i // n))
    else:
        u_shape = (m, D_SSM)
        u_spec = pl.BlockSpec((tm, D_SSM), row)
    out_shape = (
        jax.ShapeDtypeStruct(u_shape, F32),
        jax.ShapeDtypeStruct((m, 512), BF16),
        jax.ShapeDtypeStruct((m, 128), F32),
        jax.ShapeDtypeStruct((m, 128), F32),
        jax.ShapeDtypeStruct((m, 512), BF16),
        jax.ShapeDtypeStruct((m, 128), F32),
        jax.ShapeDtypeStruct((m, 128), BF16),
        jax.ShapeDtypeStruct((m, 128), BF16),
        jax.ShapeDtypeStruct((m, IDX_DIM), BF16),
    )
    out_specs = (
        u_spec,
        pl.BlockSpec((tm, 512), row), pl.BlockSpec((tm, 128), row), pl.BlockSpec((tm, 128), row),
        pl.BlockSpec((tm, 512), row), pl.BlockSpec((tm, 128), row), pl.BlockSpec((tm, 128), row),
        pl.BlockSpec((tm, 128), row), pl.BlockSpec((tm, IDX_DIM), row),
    )
    return pl.pallas_call(
        _inproj_kernel,
        grid=(m // tm,),
        in_specs=[pl.BlockSpec((tm, D_MODEL), row), _vmem_spec()],
        out_specs=out_specs,
        out_shape=out_shape,
        compiler_params=_cparams(("parallel",)),
        name="in_proj",
    )(x, w_pad)


def _s5_kernel(u_ref, h0re_ref, h0im_ref, are_ref, aim_ref, bblk_ref, cblk_ref, d_ref, wglu_ref,
               out_ref, hre_ref, him_ref, hs_ref, st_ref, y_ref, *, rows_per_step, steps):
    c = pl.program_id(0)
    rps = rows_per_step
    n_rows = rps * steps

    @pl.when(c == 0)
    def _():
        st_ref[0] = h0re_ref[...]
        st_ref[1] = h0im_ref[...]

    ub = u_ref[...].astype(BF16)
    for j in range(N_SCHUNK):
        hs_ref[j] = _dot(ub[:, UCHUNK * j:UCHUNK * (j + 1)], bblk_ref[j])

    for j in range(N_SCHUNK):
        cols = slice(SCHUNK * j, SCHUNK * (j + 1))
        a_re = jnp.broadcast_to(are_ref[:, cols], (8, SCHUNK))
        a_im = jnp.broadcast_to(aim_ref[:, cols], (8, SCHUNK))
        for rg in range(rps // 8):
            rsl = slice(8 * rg, 8 * (rg + 1))

            def step(t, carry, j=j, rg=rg, a_re=a_re, a_im=a_im):
                h_re, h_im = carry
                row = pl.multiple_of(t * rps + 8 * rg, 8)
                n_re = a_re * h_re - a_im * h_im + hs_ref[j, pl.ds(row, 8), 0:SCHUNK]
                n_im = a_re * h_im + a_im * h_re + hs_ref[j, pl.ds(row, 8), SCHUNK:2 * SCHUNK]
                hs_ref[j, pl.ds(row, 8), 0:SCHUNK] = n_re
                hs_ref[j, pl.ds(row, 8), SCHUNK:2 * SCHUNK] = n_im
                return n_re, n_im

            h_re, h_im = lax.fori_loop(0, steps, step, (st_ref[0, rsl, cols], st_ref[1, rsl, cols]),
                                       unroll=min(steps, 8))
            st_ref[0, rsl, cols] = h_re
            st_ref[1, rsl, cols] = h_im

    rc = 256
    for r0 in range(0, n_rows, rc):
        for j in range(N_SCHUNK):
            y_ref[r0:r0 + rc, UCHUNK * j:UCHUNK * (j + 1)] = _dot(
                hs_ref[j, r0:r0 + rc, :].astype(BF16), cblk_ref[j])
    y = y_ref[...] + d_ref[...] * u_ref[...]
    g = jax.nn.gelu(y).astype(BF16)
    z = _dot(g, wglu_ref[...])
    out_ref[...] = (z[:, :D_SSM] * jax.nn.sigmoid(z[:, D_SSM:])).astype(out_ref.dtype)

    @pl.when(c == pl.num_programs(0) - 1)
    def _():
        hre_ref[...] = st_ref[0]
        him_ref[...] = st_ref[1]


def _s5_call(u_t, h0_re, h0_im, a_re, a_im, bblk, cblk, d, wglu, rows_per_step, steps):
    n_rows_total = u_t.shape[0]
    n_rows = rows_per_step * steps
    kern = functools.partial(_s5_kernel, rows_per_step=rows_per_step, steps=steps)
    return pl.pallas_call(
        kern,
        grid=(n_rows_total // n_rows,),
        in_specs=[pl.BlockSpec((n_rows, D_SSM), lambda c: (c, 0))] + [_vmem_spec()] * 8,
        out_specs=(pl.BlockSpec((n_rows, D_SSM), lambda c: (c, 0)),
                   pl.BlockSpec((rows_per_step, N_STATE), lambda c: (0, 0)),
                   pl.BlockSpec((rows_per_step, N_STATE), lambda c: (0, 0))),
        out_shape=(jax.ShapeDtypeStruct((n_rows_total, D_SSM), BF16),
                   jax.ShapeDtypeStruct((rows_per_step, N_STATE), F32),
                   jax.ShapeDtypeStruct((rows_per_step, N_STATE), F32)),
        scratch_shapes=[pltpu.VMEM((N_SCHUNK, n_rows, 2 * SCHUNK), F32),
                        pltpu.VMEM((2, rows_per_step, N_STATE), F32),
                        pltpu.VMEM((n_rows, D_SSM), F32)],
        compiler_params=_cparams(("arbitrary",)),
        name="s5_mixer",
    )(u_t, h0_re, h0_im, a_re, a_im, bblk, cblk, d, wglu)


def _sortable_key(score):
    bits = pltpu.bitcast(score + 0.0, I32)
    return jnp.where(bits < 0, bits ^ 0x7FFFFFFF, bits)


TQ = 256
TK = 256


def _attn_prompt_kernel(qiT_ref, wT_ref, kib_ref, qT_ref, k_ref, vT_ref, bias_ref, o_ref,
                        keys_ref, s_ref, acc_ref, p_ref, *, topk):
    i = pl.program_id(1)
    n_kt = i + 1
    sub = lax.broadcasted_iota(I32, (TK, TQ), 0)
    lane = lax.broadcasted_iota(I32, (TK, TQ), 1)
    causal_diag = sub <= lane

    def score_body(j, carry):
        kt = kib_ref[pl.ds(pl.multiple_of(j * TK, TK), TK), :]
        acc = jnp.zeros((TK, TQ), F32)
        for h in range(N_IDX_HEADS):
            s = _dot(kt, qiT_ref[IDX_DIM * h:IDX_DIM * (h + 1), :])
            acc = acc + jnp.maximum(s, 0.0) * wT_ref[h:h + 1, :]
        key = _sortable_key(acc * IDX_SCALE)
        ok = jnp.logical_or(j < i, causal_diag)
        keys_ref[j] = jnp.where(ok, key, INT_MIN)
        return carry

    lax.fori_loop(0, n_kt, score_body, 0)

    def count(pred_fn):
        def body(j, c):
            hit = pred_fn(j, keys_ref[j]).astype(I32)
            return c + jnp.sum(hit.reshape(TK // 8, 8, TQ), axis=0)
        c8 = lax.fori_loop(0, n_kt, body, jnp.zeros((8, TQ), I32))
        return jnp.sum(c8, axis=0, keepdims=True)

    zero = jnp.zeros((1, TQ), I32)
    thr = jnp.where(count(lambda j, kk: kk >= zero) >= topk, zero, jnp.full((1, TQ), INT_MIN, I32))

    def bit_body(bi, thr):
        cand = thr | jnp.left_shift(jnp.int32(1), 30 - bi)
        return jnp.where(count(lambda j, kk: kk >= cand) >= topk, cand, thr)

    thr = lax.fori_loop(0, 31, bit_body, thr)

    cnt_ge = count(lambda j, kk: kk >= thr)
    p_ref[...] = jnp.full((1, TQ), 2 ** 30, I32)

    @pl.when(jnp.max(cnt_ge) > topk)
    def _():
        need = topk - count(lambda j, kk: kk > thr)

        def pos_body(bi, pos):
            cand = pos | jnp.left_shift(jnp.int32(1), 10 - bi)
            before = count(lambda j, kk: jnp.logical_and(kk == thr, sub + j * TK < cand))
            return jnp.where(before < need, cand, pos)

        p_ref[...] = lax.fori_loop(0, 11, pos_body, zero)

    pos = p_ref[...]
    acc_ref[...] = jnp.zeros(acc_ref.shape, F32)

    def logits_body(j, mx):
        kk = keys_ref[j]
        tie_ok = jnp.logical_and(kk == thr, sub + j * TK <= pos)
        sel = jnp.logical_and(jnp.logical_or(kk > thr, tie_ok), kk != INT_MIN)
        neg = jnp.where(sel, 0.0, NEG_BIG)
        kind = jnp.minimum(i - j, 2)
        row0 = pl.multiple_of(j * TK, TK)
        new = []
        for h in range(N_HEADS):
            g = h // N_REP
            s = _dot(k_ref[g, pl.ds(row0, TK), :], qT_ref[HEAD_DIM * h:HEAD_DIM * (h + 1), :])
            s = s + bias_ref[h, kind] + neg
            s_ref[h, j] = s
            new.append(jnp.maximum(mx[h], jnp.max(s.reshape(TK // 8, 8, TQ), axis=0)))
        return tuple(new)

    mx = lax.fori_loop(0, n_kt, logits_body, tuple(jnp.full((8, TQ), NEG_BIG, F32) for _ in range(N_HEADS)))
    mx = tuple(jnp.broadcast_to(jnp.max(m, axis=0, keepdims=True), (8, TQ)) for m in mx)

    def pv_body(j, ls):
        new = []
        for h in range(N_HEADS):
            g = h // N_REP
            hs = slice(HEAD_DIM * h, HEAD_DIM * (h + 1))
            p = jnp.exp(s_ref[h, j].reshape(TK // 8, 8, TQ) - mx[h][None])
            new.append(ls[h] + jnp.sum(p, axis=0))
            pv = _dot(vT_ref[j, HEAD_DIM * g:HEAD_DIM * (g + 1), :], p.reshape(TK, TQ).astype(BF16))
            acc_ref[hs, :] = acc_ref[hs, :] + pv
        return tuple(new)

    ls = lax.fori_loop(0, n_kt, pv_body, tuple(jnp.zeros((8, TQ), F32) for _ in range(N_HEADS)))

    for h in range(N_HEADS):
        hs = slice(HEAD_DIM * h, HEAD_DIM * (h + 1))
        acc_ref[hs, :] = acc_ref[hs, :] / jnp.sum(ls[h], axis=0, keepdims=True)
    o_ref[...] = acc_ref[...].T.astype(o_ref.dtype)


def _attn_prompt_call(qiT, wT, kib, qT, k2, vT, biasT, bsz, seq, topk):
    nq = seq // TQ
    nkt = seq // TK
    kern = functools.partial(_attn_prompt_kernel, topk=topk)
    return pl.pallas_call(
        kern,
        grid=(bsz, nq),
        in_specs=[
            pl.BlockSpec((512, TQ), lambda b, i: (0, b * nq + i)),
            pl.BlockSpec((N_IDX_HEADS, TQ), lambda b, i: (0, b * nq + i)),
            pl.BlockSpec((seq, IDX_DIM), lambda b, i: (b, 0)),
            pl.BlockSpec((512, TQ), lambda b, i: (0, b * nq + i)),
            pl.BlockSpec((N_KV_HEADS, seq, HEAD_DIM), lambda b, i: (0, b, 0)),
            pl.BlockSpec((None, nkt, 128, TK), lambda b, i: (b, 0, 0, 0)),
            _vmem_spec(),
        ],
        out_specs=pl.BlockSpec((TQ, D_ATTN), lambda b, i: (b * nq + i, 0)),
        out_shape=jax.ShapeDtypeStruct((bsz * seq, D_ATTN), BF16),
        scratch_shapes=[pltpu.VMEM((nkt, TK, TQ), I32),
                        pltpu.VMEM((N_HEADS, nkt, TK, TQ), F32),
                        pltpu.VMEM((D_ATTN, TQ), F32),
                        pltpu.VMEM((1, TQ), I32)],
        compiler_params=_cparams(("parallel", "arbitrary")),
        name="attn_prompt",
    )(qiT, wT, kib, qT, k2, vT, biasT)


NB = 8
N_PAGES = 16
PAST = N_PAGES * PAGE_SIZE
S_ALL = PAST + PAGE_SIZE
T_DEC = 4
QROWS = T_DEC * N_HEADS
_NT = (((1,), (1,)), ((), ()))


def _kidx_copies(pt_ref, ckiT_hbm, kiT_buf, sem, layer, step, slot):
    copies = []
    for n in range(NB):
        for pg in range(N_PAGES):
            page = pt_ref[step * NB + n, pg]
            dst = kiT_buf.at[slot, n, :, pl.ds(pg * PAGE_SIZE, PAGE_SIZE)]
            copies.append(pltpu.make_async_copy(ckiT_hbm.at[layer, page], dst, sem.at[slot]))
    return copies


def _kv_copies(pt_ref, ckT_hbm, cvT_hbm, kT_buf, vT_buf, sem, layer, b, kslot):
    copies = []
    for pg in range(N_PAGES):
        page = pt_ref[b, pg]
        cols = pl.ds(pg * PAGE_SIZE, PAGE_SIZE)
        copies.append(pltpu.make_async_copy(ckT_hbm.at[layer, page], kT_buf.at[kslot, :, cols], sem.at[0, kslot]))
        copies.append(pltpu.make_async_copy(cvT_hbm.at[layer, page], vT_buf.at[kslot, :, cols], sem.at[1, kslot]))
    return copies


def _attn_sample_kernel(pt_ref, qi_ref, w_ref, q_ref, kinT_ref, knT_ref, vnT_ref, bias_ref,
                        ckT_hbm, cvT_hbm, ckiT_hbm, o_ref,
                        kT_buf, vT_buf, kiT_buf, keys_ref, sel_ref, prob_ref, sem_ki, sem_kv, *, layer, topk):
    s = pl.program_id(0)
    n_steps = pl.num_programs(0)
    slot = lax.rem(s, 2)
    b0 = s * NB
    ki_fetch = functools.partial(_kidx_copies, pt_ref, ckiT_hbm, kiT_buf, sem_ki, layer)
    kv_fetch = functools.partial(_kv_copies, pt_ref, ckT_hbm, cvT_hbm, kT_buf, vT_buf, sem_kv, layer)

    @pl.when(s == 0)
    def _():
        for cp in ki_fetch(0, 0):
            cp.start()

    for cp in kv_fetch(b0, 0):
        cp.start()

    @pl.when(s + 1 < n_steps)
    def _():
        for cp in ki_fetch(s + 1, 1 - slot):
            cp.start()

    for cp in ki_fetch(s, slot):
        cp.wait()

    lane = lax.broadcasted_iota(I32, (1, S_ALL), 1)

    for n in range(NB):
        qi = qi_ref[n]
        sc = jnp.concatenate([_dot(qi, kiT_buf[slot, n].astype(BF16)), _dot(qi, kinT_ref[n])], axis=1)
        sc = jnp.maximum(sc, 0.0) * w_ref[n]
        for t in range(T_DEC):
            row = jnp.sum(sc[N_IDX_HEADS * t:N_IDX_HEADS * (t + 1), :], axis=0, keepdims=True) * IDX_SCALE
            key = jnp.where(lane <= PAST + t, _sortable_key(row), INT_MIN)
            keys_ref[T_DEC * n + t:T_DEC * n + t + 1, :] = key

    n_grp = NB * T_DEC // 8
    grp = [slice(8 * g, 8 * (g + 1)) for g in range(n_grp)]

    def count(hit):
        return jnp.sum(hit.astype(I32), axis=1, keepdims=True)

    zero = jnp.zeros((8, 1), I32)
    thr = tuple(jnp.where(count(keys_ref[gs, :] >= zero) >= topk, zero, jnp.full((8, 1), INT_MIN, I32))
                for gs in grp)

    def bit_body(bi, thr):
        bit = jnp.left_shift(jnp.int32(1), 30 - bi)
        out = []
        for g, gs in enumerate(grp):
            cand = thr[g] | bit
            out.append(jnp.where(count(keys_ref[gs, :] >= cand) >= topk, cand, thr[g]))
        return tuple(out)

    thr = lax.fori_loop(0, 31, bit_body, thr)
    need = tuple(topk - count(keys_ref[gs, :] > thr[g]) for g, gs in enumerate(grp))
    lane8 = lax.broadcasted_iota(I32, (8, S_ALL), 1)

    def pos_body(bi, pos):
        bit = jnp.left_shift(jnp.int32(1), 11 - bi)
        out = []
        for g, gs in enumerate(grp):
            cand = pos[g] | bit
            before = count(jnp.logical_and(keys_ref[gs, :] == thr[g], lane8 < cand))
            out.append(jnp.where(before < need[g], cand, pos[g]))
        return tuple(out)

    pos = lax.fori_loop(0, 12, pos_body, tuple(zero for _ in grp))
    for g, gs in enumerate(grp):
        kk = keys_ref[gs, :]
        tie_ok = jnp.logical_and(kk == thr[g], lane8 <= pos[g])
        sel = jnp.logical_and(jnp.logical_or(kk > thr[g], tie_ok), kk != INT_MIN).astype(I32)
        sel_ref[2 * g] = sel[0:T_DEC]
        sel_ref[2 * g + 1] = sel[T_DEC:2 * T_DEC]

    def attend_body(n, carry):
        kslot = lax.rem(n, 2)

        @pl.when(n + 1 < NB)
        def _():
            for cp in kv_fetch(b0 + n + 1, 1 - kslot):
                cp.start()

        for cp in kv_fetch(b0 + n, kslot):
            cp.wait()

        q = q_ref[n]
        lg = jnp.concatenate([_dot(q, kT_buf[kslot].astype(BF16)), _dot(q, knT_ref[n])], axis=1) + bias_ref[...]
        sel_n = sel_ref[n]
        for t in range(T_DEC):
            rs = slice(N_HEADS * t, N_HEADS * (t + 1))
            x = jnp.where(sel_n[t:t + 1, :] > 0, lg[rs, :], NEG_BIG)
            mx = jnp.max(x, axis=1, keepdims=True)
            e = jnp.exp(x - mx)
            prob_ref[rs, :] = e / jnp.sum(e, axis=1, keepdims=True)
        p = prob_ref[...].astype(BF16)
        o_ref[n] = (lax.dot_general(p[:, :PAST], vT_buf[kslot].astype(BF16), _NT, preferred_element_type=F32)
                    + lax.dot_general(p[:, PAST:], vnT_ref[n], _NT, preferred_element_type=F32))
        return carry

    lax.fori_loop(0, NB, attend_body, 0)


def _attn_sample_call(page_table, qi_s, w_s, q_s, kinT, knT, vnT, bias_s, ckT, cvT, ckiT, layer, topk):
    bsz = qi_s.shape[0]
    kern = functools.partial(_attn_sample_kernel, layer=layer, topk=topk)
    blk3 = lambda shp: pl.BlockSpec((NB,) + shp, lambda s, pt: (s, 0, 0))
    any_spec = pl.BlockSpec(memory_space=pl.ANY)
    grid_spec = pltpu.PrefetchScalarGridSpec(
        num_scalar_prefetch=1,
        grid=(bsz // NB,),
        in_specs=[blk3((QROWS, IDX_DIM)), blk3((QROWS, 1)), blk3((QROWS, 128)),
                  blk3((IDX_DIM, PAGE_SIZE)), blk3((128, PAGE_SIZE)), blk3((128, PAGE_SIZE)),
                  pl.BlockSpec((QROWS, S_ALL), lambda s, pt: (0, 0)),
                  any_spec, any_spec, any_spec],
        out_specs=blk3((QROWS, 128)),
        scratch_shapes=[pltpu.VMEM((2, 128, PAST), F32),
                        pltpu.VMEM((2, 128, PAST), F32),
                        pltpu.VMEM((2, NB, IDX_DIM, PAST), F32),
                        pltpu.VMEM((NB * T_DEC, S_ALL), I32),
                        pltpu.VMEM((NB, T_DEC, S_ALL), I32),
                        pltpu.VMEM((QROWS, S_ALL), F32),
                        pltpu.SemaphoreType.DMA((2,)),
                        pltpu.SemaphoreType.DMA((2, 2))],
    )
    return pl.pallas_call(
        kern,
        grid_spec=grid_spec,
        out_shape=jax.ShapeDtypeStruct((bsz, QROWS, 128), F32),
        compiler_params=_cparams(("arbitrary",)),
        name="attn_sample",
    )(page_table, qi_s, w_s, q_s, kinT, knT, vnT, bias_s, ckT, cvT, ckiT)


def _outproj_kernel(x_ref, ssm_ref, attn_ref, wt_ref, wb_ref, g_ref, b_ref, o_ref):
    mixed = _dot(ssm_ref[...], wt_ref[...]) + _dot(attn_ref[...], wb_ref[...])
    o_ref[...] = _layer_norm(ALPHA * x_ref[...] + mixed, g_ref[...], b_ref[...])


def _outproj_call(x, ssm, attn, w_top, w_bot, g, b, tm, ssm_tmajor_tiles):
    m = x.shape[0]
    row = lambda i: (i, 0)
    if ssm_tmajor_tiles:
        n = ssm_tmajor_tiles
        ssm_spec = pl.BlockSpec((tm, D_SSM), lambda i: (i % n, i // n))
    else:
        ssm_spec = pl.BlockSpec((tm, D_SSM), row)
    return pl.pallas_call(
        _outproj_kernel,
        grid=(m // tm,),
        in_specs=[pl.BlockSpec((tm, D_MODEL), row), ssm_spec, pl.BlockSpec((tm, D_ATTN), row),
                  _vmem_spec(), _vmem_spec(), _vmem_spec(), _vmem_spec()],
        out_specs=pl.BlockSpec((tm, D_MODEL), row),
        out_shape=jax.ShapeDtypeStruct((m, D_MODEL), F32),
        compiler_params=_cparams(("parallel",)),
        name="out_proj_ln1",
    )(x, ssm, attn, w_top, w_bot, g, b)


def _ffn_kernel(x_ref, wg_ref, wu_ref, wd_ref, o_ref, xb_ref, acc_ref):
    f = pl.program_id(1)

    @pl.when(f == 0)
    def _():
        xb_ref[...] = x_ref[...].astype(BF16)
        acc_ref[...] = jnp.zeros(acc_ref.shape, F32)

    xb = xb_ref[...]
    h = jax.nn.silu(_dot(xb, wg_ref[...])) * _dot(xb, wu_ref[...])
    acc_ref[...] += _dot(h.astype(BF16), wd_ref[...])

    @pl.when(f == pl.num_programs(1) - 1)
    def _():
        o_ref[...] = acc_ref[...]


def _ffn_call(x, wg, wu, wd, tm, tf):
    m = x.shape[0]
    dff = wg.shape[1]
    return pl.pallas_call(
        _ffn_kernel,
        grid=(m // tm, dff // tf),
        in_specs=[pl.BlockSpec((tm, D_MODEL), lambda i, f: (i, 0)),
                  pl.BlockSpec((D_MODEL, tf), lambda i, f: (0, f)),
                  pl.BlockSpec((D_MODEL, tf), lambda i, f: (0, f)),
                  pl.BlockSpec((tf, D_MODEL), lambda i, f: (f, 0))],
        out_specs=pl.BlockSpec((tm, D_MODEL), lambda i, f: (i, 0)),
        out_shape=jax.ShapeDtypeStruct((m, D_MODEL), F32),
        scratch_shapes=[pltpu.VMEM((tm, D_MODEL), BF16), pltpu.VMEM((tm, D_MODEL), F32)],
        compiler_params=_cparams(("parallel", "arbitrary")),
        name="ffn_swiglu",
    )(x, wg, wu, wd)


def _router_kernel(x_ref, wr_ref, br_ref, g_ref):
    logits = jnp.dot(x_ref[...], wr_ref[...], preferred_element_type=F32,
                     precision=lax.Precision.HIGHEST) + br_ref[...]
    lane = lax.broadcasted_iota(I32, logits.shape, 1)
    logits = jnp.where(lane < N_EXPERTS, logits, -jnp.inf)
    lane_f = lane.astype(F32)
    m1 = jnp.max(logits, axis=1, keepdims=True)
    i1 = jnp.min(jnp.where(logits == m1, lane_f, 128.0), axis=1, keepdims=True)
    rest = jnp.where(lane_f == i1, -jnp.inf, logits)
    m2 = jnp.max(rest, axis=1, keepdims=True)
    i2 = jnp.min(jnp.where(rest == m2, lane_f, 128.0), axis=1, keepdims=True)
    e2 = jnp.exp(m2 - m1)
    den = 1.0 + e2
    g_ref[...] = jnp.where(lane_f == i1, 1.0 / den, 0.0) + jnp.where(lane_f == i2, e2 / den, 0.0)


def _router_call(x, wr_pad, br_pad, tm):
    m = x.shape[0]
    return pl.pallas_call(
        _router_kernel,
        grid=(m // tm,),
        in_specs=[pl.BlockSpec((tm, D_MODEL), lambda i: (i, 0)), _vmem_spec(), _vmem_spec()],
        out_specs=pl.BlockSpec((tm, 128), lambda i: (i, 0)),
        out_shape=jax.ShapeDtypeStruct((m, 128), F32),
        compiler_params=_cparams(("parallel",)),
        name="moe_router",
    )(x, wr_pad, br_pad)


def _moe_kernel(x_ref, gate_ref, wg_ref, wu_ref, wd_ref, o_ref, xb_ref, acc_ref):
    e = pl.program_id(1)

    @pl.when(e == 0)
    def _():
        xb_ref[...] = x_ref[...].astype(BF16)
        acc_ref[...] = jnp.zeros(acc_ref.shape, F32)

    xb = xb_ref[...]
    h = jax.nn.silu(_dot(xb, wg_ref[...])) * _dot(xb, wu_ref[...])
    y = _dot(h.astype(BF16), wd_ref[...])
    gates = gate_ref[...]
    lane = lax.broadcasted_iota(I32, gates.shape, 1)
    ge = jnp.sum(jnp.where(lane == e, gates, 0.0), axis=1, keepdims=True)
    acc_ref[...] += ge * y

    @pl.when(e == pl.num_programs(1) - 1)
    def _():
        o_ref[...] = acc_ref[...]


def _moe_call(x, gates, wg, wu, wd, tm):
    m = x.shape[0]
    return pl.pallas_call(
        _moe_kernel,
        grid=(m // tm, N_EXPERTS),
        in_specs=[pl.BlockSpec((tm, D_MODEL), lambda i, e: (i, 0)),
                  pl.BlockSpec((tm, 128), lambda i, e: (i, 0)),
                  pl.BlockSpec((None, D_MODEL, D_FF_EXPERT), lambda i, e: (e, 0, 0)),
                  pl.BlockSpec((None, D_MODEL, D_FF_EXPERT), lambda i, e: (e, 0, 0)),
                  pl.BlockSpec((None, D_FF_EXPERT, D_MODEL), lambda i, e: (e, 0, 0))],
        out_specs=pl.BlockSpec((tm, D_MODEL), lambda i, e: (i, 0)),
        out_shape=jax.ShapeDtypeStruct((m, D_MODEL), F32),
        scratch_shapes=[pltpu.VMEM((tm, D_MODEL), BF16), pltpu.VMEM((tm, D_MODEL), F32)],
        compiler_params=_cparams(("parallel", "arbitrary")),
        name="moe_swiglu",
    )(x, gates, wg, wu, wd)


def _ple_ln2_kernel(x_ref, cm_ref, p_ref, wpg_ref, wpp_ref, g_ref, b_ref, o_ref):
    x = x_ref[...]
    gate = jax.nn.sigmoid(_dot(x.astype(BF16), wpg_ref[...]))
    ple = gate * _dot(p_ref[...].astype(BF16), wpp_ref[...])
    o_ref[...] = _layer_norm(ALPHA * x + cm_ref[...] + ple, g_ref[...], b_ref[...])


def _ple_ln2_call(x, cm, p, wpg, wpp, g, b, tm):
    m = x.shape[0]
    row = lambda i: (i, 0)
    return pl.pallas_call(
        _ple_ln2_kernel,
        grid=(m // tm,),
        in_specs=[pl.BlockSpec((tm, D_MODEL), row), pl.BlockSpec((tm, D_MODEL), row),
                  pl.BlockSpec((tm, D_PLE), row), _vmem_spec(), _vmem_spec(), _vmem_spec(), _vmem_spec()],
        out_specs=pl.BlockSpec((tm, D_MODEL), row),
        out_shape=jax.ShapeDtypeStruct((m, D_MODEL), F32),
        compiler_params=_cparams(("parallel",)),
        name="ple_ln2",
    )(x, cm, p, wpg, wpp, g, b)


def _t5_bucket(dist):
    max_exact = N_BUCKETS // 2
    n = jnp.maximum(dist, 0)
    nf = jnp.maximum(n, 1).astype(F32)
    large = max_exact + (jnp.log(nf / max_exact) / math.log(MAX_DISTANCE / max_exact)
                         * (N_BUCKETS - max_exact)).astype(I32)
    large = jnp.minimum(large, N_BUCKETS - 1)
    return jnp.where(n < max_exact, n, large)


def _ssm_matrices(a_re, a_im, log_dt, b_re, b_im, c_re, c_im):
    lam = lax.complex(a_re, a_im)
    dt = jnp.exp(log_dt)[:, None]
    a_bar = jnp.exp(lam * dt)
    b_bar = ((a_bar - 1.0) / lam)[:, :, None] * lax.complex(b_re, b_im)
    gpc = N_SSM_GROUPS // N_SCHUNK
    eye = jnp.eye(gpc, dtype=F32)

    def in_block(x):
        x = x.reshape(N_SCHUNK, gpc, SSM_STATE, SSM_GROUP)
        blk = jnp.einsum('jgnc,gh->jgchn', x, eye)
        return blk.reshape(N_SCHUNK, gpc * SSM_GROUP, gpc * SSM_STATE)

    def out_block(x):
        x = x.reshape(N_SCHUNK, gpc, SSM_GROUP, SSM_STATE)
        blk = jnp.einsum('jgcn,gh->jgnhc', x, eye)
        return blk.reshape(N_SCHUNK, gpc * SSM_STATE, gpc * SSM_GROUP)

    bblk = jnp.concatenate([in_block(b_bar.real), in_block(b_bar.imag)], axis=2).astype(BF16)
    cblk = jnp.concatenate([out_block(c_re), out_block(-c_im)], axis=1).astype(BF16)
    return (a_bar.real.reshape(1, N_STATE), a_bar.imag.reshape(1, N_STATE), bblk, cblk)


def _bias_lookup(rel_bias, dist):
    bucket = _t5_bucket(jnp.asarray(dist))[None]
    out = jnp.zeros((N_HEADS,) + dist.shape, F32)
    for b in range(N_BUCKETS):
        out = jnp.where(bucket == b, rel_bias[b].reshape((N_HEADS,) + (1,) * dist.ndim), out)
    return out


def _prompt_bias_tiles(rel_bias):
    c = np.arange(TK)[:, None]
    r = np.arange(TQ)[None, :]
    dist = np.stack([kind * TQ + r - c for kind in range(3)]).astype(np.int32)
    return _bias_lookup(rel_bias, dist)


def _sample_bias_rows(rel_bias):
    key_pos = np.arange(S_ALL)[None, :]
    q_pos = PAST + np.arange(T_DEC)[:, None]
    dist = (q_pos - key_pos).astype(np.int32)
    return jnp.transpose(_bias_lookup(rel_bias, dist), (1, 0, 2)).reshape(QROWS, S_ALL)


def kernel(x_prompt, x_sample, p_prompt, p_sample, cache_k, cache_v, cache_kidx, state_ssm_re, state_ssm_im, page_table, ln_emb_g, ln_emb_b, w_in, ssm_a_re, ssm_a_im, ssm_log_dt, ssm_b_re, ssm_b_im, ssm_c_re, ssm_c_im, ssm_d, ssm_w_glu, rel_bias, w_out, ln1_g, ln1_b, ffn_w_gate, ffn_w_up, ffn_w_down, moe_w_router, moe_b_router, moe_w_gate, moe_w_up, moe_w_down, ple_w_gate, ple_w_proj, ln2_g, ln2_b):
    bsz, seq, _ = x_prompt.shape
    dbsz, dseq, _ = x_sample.shape
    assert dseq == T_DEC and page_table.shape[1] == N_PAGES and dbsz % NB == 0
    mp = bsz * seq
    ms = dbsz * dseq
    tm = 512
    tiles_per_seq = seq // tm
    topk_p = min(TOPK_MAX, seq // 4)
    topk_s = min(TOPK_MAX, (PAST + dseq) // 4)
    n_pool = cache_k.shape[1]

    row2 = lambda a: a.reshape(1, -1)
    xp = _ln_call(x_prompt.reshape(mp, D_MODEL), row2(ln_emb_g), row2(ln_emb_b), tm)
    xs = _ln_call(x_sample.reshape(ms, D_MODEL), row2(ln_emb_g), row2(ln_emb_b), tm)

    ckT = jnp.transpose(cache_k, (0, 1, 3, 4, 2)).reshape(DEPTH, n_pool, 128, PAGE_SIZE)
    cvT = jnp.transpose(cache_v, (0, 1, 3, 4, 2)).reshape(DEPTH, n_pool, 128, PAGE_SIZE)
    ckiT = jnp.transpose(cache_kidx, (0, 1, 3, 2))
    bias_p = _prompt_bias_tiles(rel_bias)
    bias_s = _sample_bias_rows(rel_bias)
    h0_prompt = jnp.zeros((bsz, N_STATE), F32)

    outs = [[] for _ in range(10)]
    for i in range(DEPTH):
        w_pad = jnp.pad(w_in[i], ((0, 0), (0, D_IN_PAD - D_IN))).astype(BF16)
        a_re, a_im, bblk, cblk = _ssm_matrices(ssm_a_re[i], ssm_a_im[i], ssm_log_dt[i], ssm_b_re[i], ssm_b_im[i],
                                               ssm_c_re[i], ssm_c_im[i])
        d_row = row2(ssm_d[i])
        wglu = ssm_w_glu[i].astype(BF16)
        w_top = w_out[i, :D_SSM].astype(BF16)
        w_bot = w_out[i, D_SSM:].astype(BF16)
        wpg = ple_w_gate[i].astype(BF16)
        wpp = ple_w_proj[i].astype(BF16)
        j = i // 2

        def channel_mixer(x1, tm_cm):
            if i % 2 == 0:
                return _ffn_call(x1, ffn_w_gate[j].astype(BF16), ffn_w_up[j].astype(BF16),
                                 ffn_w_down[j].astype(BF16), tm_cm, D_FF // 2)
            wr_pad = jnp.pad(moe_w_router[j], ((0, 0), (0, 128 - N_EXPERTS)))
            br_pad = jnp.pad(moe_b_router[j], (0, 128 - N_EXPERTS)).reshape(1, 128)
            gates = _router_call(x1, wr_pad, br_pad, 512)
            return _moe_call(x1, gates, moe_w_gate[j].astype(BF16), moe_w_up[j].astype(BF16),
                             moe_w_down[j].astype(BF16), tm_cm)

        u_t, q, k, v, qi, tail, kb, vb, kib = _inproj_call(xp, w_pad, tm, tiles_per_seq)
        ssm_t, hre, him = _s5_call(u_t.reshape(seq * bsz, D_SSM), h0_prompt, h0_prompt, a_re, a_im, bblk, cblk,
                                   d_row, wglu, rows_per_step=bsz, steps=128)
        k2 = jnp.transpose(kb.reshape(mp, N_KV_HEADS, HEAD_DIM), (1, 0, 2))
        vT = jnp.transpose(vb.reshape(bsz, seq // TK, TK, 128), (0, 1, 3, 2))
        attn = _attn_prompt_call(qi.T, tail[:, IDX_DIM:IDX_DIM + N_IDX_HEADS].T, kib, q.T, k2, vT, bias_p,
                                 bsz, seq, topk_p)
        x1 = _outproj_call(xp, ssm_t.reshape(seq, bsz * D_SSM), attn, w_top, w_bot, row2(ln1_g[i]), row2(ln1_b[i]),
                           tm, tiles_per_seq)
        cm = channel_mixer(x1, 512)
        xp = _ple_ln2_call(x1, cm, p_prompt[i].reshape(mp, D_PLE), wpg, wpp, row2(ln2_g[i]), row2(ln2_b[i]), tm)
        outs[0].append(k.reshape(bsz, seq, N_KV_HEADS, HEAD_DIM))
        outs[1].append(v.reshape(bsz, seq, N_KV_HEADS, HEAD_DIM))
        outs[2].append(tail[:, :IDX_DIM].reshape(bsz, seq, IDX_DIM))
        outs[3].append(hre.reshape(bsz, N_SSM_GROUPS, SSM_STATE))
        outs[4].append(him.reshape(bsz, N_SSM_GROUPS, SSM_STATE))

        u, q, k, v, qi, tail, kb, vb, kib = _inproj_call(xs, w_pad, tm, 0)
        u_t = jnp.transpose(u.reshape(dbsz, dseq, D_SSM), (1, 0, 2)).reshape(ms, D_SSM)
        ssm_t, hre, him = _s5_call(u_t, state_ssm_re[i].reshape(dbsz, N_STATE), state_ssm_im[i].reshape(dbsz, N_STATE),
                                   a_re, a_im, bblk, cblk, d_row, wglu, rows_per_step=dbsz, steps=dseq)
        ssm = jnp.transpose(ssm_t.reshape(dseq, dbsz, D_SSM), (1, 0, 2)).reshape(ms, D_SSM)
        q4 = q.reshape(dbsz, dseq, N_KV_HEADS, N_REP, HEAD_DIM)
        q_exp = jnp.einsum('btgrd,gh->btgrhd', q4, jnp.eye(N_KV_HEADS, dtype=BF16)).reshape(dbsz, QROWS, 128)
        pad_new = lambda a: jnp.transpose(
            jnp.pad(a.reshape(dbsz, dseq, -1), ((0, 0), (0, PAGE_SIZE - dseq), (0, 0))), (0, 2, 1))
        o_s = _attn_sample_call(page_table, qi.reshape(dbsz, QROWS, IDX_DIM),
                                tail[:, IDX_DIM:IDX_DIM + N_IDX_HEADS].reshape(dbsz, QROWS, 1), q_exp,
                                pad_new(kib), pad_new(kb), pad_new(vb), bias_s,
                                ckT, cvT, ckiT, i, topk_s)
        o5 = o_s.reshape(dbsz, dseq, N_KV_HEADS, N_REP, N_KV_HEADS, HEAD_DIM)
        attn = jnp.einsum('btgrhd,gh->btgrd', o5, jnp.eye(N_KV_HEADS, dtype=F32)).reshape(ms, D_ATTN).astype(BF16)
        x1 = _outproj_call(xs, ssm, attn, w_top, w_bot, row2(ln1_g[i]), row2(ln1_b[i]), tm, 0)
        cm = channel_mixer(x1, 512)
        xs = _ple_ln2_call(x1, cm, p_sample[i].reshape(ms, D_PLE), wpg, wpp, row2(ln2_g[i]), row2(ln2_b[i]), tm)
        outs[5].append(k.reshape(dbsz, dseq, N_KV_HEADS, HEAD_DIM))
        outs[6].append(v.reshape(dbsz, dseq, N_KV_HEADS, HEAD_DIM))
        outs[7].append(tail[:, :IDX_DIM].reshape(dbsz, dseq, IDX_DIM))
        outs[8].append(hre.reshape(dbsz, N_SSM_GROUPS, SSM_STATE))
        outs[9].append(him.reshape(dbsz, N_SSM_GROUPS, SSM_STATE))

    return (xp.reshape(bsz, seq, D_MODEL), xs.reshape(dbsz, dseq, D_MODEL)) + tuple(jnp.stack(o) for o in outs)
```

```python
import functools
import math

import numpy as np
import jax
import jax.numpy as jnp
from jax import lax
from jax.experimental import pallas as pl
from jax.experimental.pallas import tpu as pltpu

F32 = jnp.float32
BF16 = jnp.bfloat16
I32 = jnp.int32

D_MODEL = 1024
DEPTH = 4
PAGE_SIZE = 128
D_SSM = 512
D_ATTN = 512
SSM_GROUP = 16
N_SSM_GROUPS = 32
SSM_STATE = 64
HEAD_DIM = 64
N_HEADS = 8
N_KV_HEADS = 2
N_REP = 4
N_IDX_HEADS = 8
IDX_DIM = 64
IDX_SCALE = (IDX_DIM ** -0.5) * (N_IDX_HEADS ** -0.5)
TOPK_MAX = 256
N_BUCKETS = 32
MAX_DISTANCE = 128
D_FF = 2816
N_EXPERTS = 8
D_FF_EXPERT = 1408
D_PLE = 256
LN_EPS = 1e-5
ALPHA = (2 * DEPTH) ** 0.25
D_IN = 1864
D_IN_PAD = 1920
N_STATE = N_SSM_GROUPS * SSM_STATE
N_SCHUNK = 4
SCHUNK = N_STATE // N_SCHUNK
UCHUNK = D_SSM // N_SCHUNK

INT_MIN = -2 ** 31
NEG_BIG = -1e30
VMEM_LIMIT = 56 * 1024 * 1024


def _cparams(sem):
    return pltpu.CompilerParams(dimension_semantics=sem, vmem_limit_bytes=VMEM_LIMIT)


def _vmem_spec():
    return pl.BlockSpec(memory_space=pltpu.VMEM)


def _layer_norm(x, g, b):
    mu = jnp.mean(x, axis=-1, keepdims=True)
    xc = x - mu
    var = jnp.mean(xc * xc, axis=-1, keepdims=True)
    return xc * lax.rsqrt(var + LN_EPS) * g + b


def _dot(a, b):
    return jnp.dot(a, b, preferred_element_type=F32)


def _ln_kernel(x_ref, g_ref, b_ref, o_ref):
    o_ref[...] = _layer_norm(x_ref[...], g_ref[...], b_ref[...])


def _ln_call(x, g, b, tm):
    m = x.shape[0]
    return pl.pallas_call(
        _ln_kernel,
        grid=(m // tm,),
        in_specs=[pl.BlockSpec((tm, D_MODEL), lambda i: (i, 0)), _vmem_spec(), _vmem_spec()],
        out_specs=pl.BlockSpec((tm, D_MODEL), lambda i: (i, 0)),
        out_shape=jax.ShapeDtypeStruct((m, D_MODEL), F32),
        compiler_params=_cparams(("parallel",)),
        name="ln_embed",
    )(x, g, b)


def _inproj_kernel(x_ref, w_ref, u_ref, q_ref, k_ref, v_ref, qi_ref, tail_ref, kb_ref, vb_ref, kib_ref):
    xb = x_ref[...].astype(BF16)
    u_ref[...] = _dot(xb, w_ref[:, 0:512])
    q_ref[...] = (_dot(xb, w_ref[:, 512:1024]) * (HEAD_DIM ** -0.5)).astype(BF16)
    k = _dot(xb, w_ref[:, 1024:1152])
    v = _dot(xb, w_ref[:, 1152:1280])
    k_ref[...] = k
    v_ref[...] = v
    kb_ref[...] = k.astype(BF16)
    vb_ref[...] = v.astype(BF16)
    qi_ref[...] = _dot(xb, w_ref[:, 1280:1792]).astype(BF16)
    tail = _dot(xb, w_ref[:, 1792:1920])
    tail_ref[...] = tail
    kib_ref[...] = tail[:, 0:IDX_DIM].astype(BF16)


def _inproj_call(x, w_pad, tm, u_tmajor_tiles):
    m = x.shape[0]
    row = lambda i: (i, 0)
    if u_tmajor_tiles:
        n = u_tmajor_tiles
        u_shape = (n * tm, (m // (n * tm)) * D_SSM)
        u_spec = pl.BlockSpec((tm, D_SSM), lambda i: (i % n, i // n))
    else:
        u_shape = (m, D_SSM)
        u_spec = pl.BlockSpec((tm, D_SSM), row)
    out_shape = (
        jax.ShapeDtypeStruct(u_shape, F32),
        jax.ShapeDtypeStruct((m, 512), BF16),
        jax.ShapeDtypeStruct((m, 128), F32),
        jax.ShapeDtypeStruct((m, 128), F32),
        jax.ShapeDtypeStruct((m, 512), BF16),
        jax.ShapeDtypeStruct((m, 128), F32),
        jax.ShapeDtypeStruct((m, 128), BF16),
        jax.ShapeDtypeStruct((m, 128), BF16),
        jax.ShapeDtypeStruct((m, IDX_DIM), BF16),
    )
    out_specs = (
        u_spec,
        pl.BlockSpec((tm, 512), row), pl.BlockSpec((tm, 128), row), pl.BlockSpec((tm, 128), row),
        pl.BlockSpec((tm, 512), row), pl.BlockSpec((tm, 128), row), pl.BlockSpec((tm, 128), row),
        pl.BlockSpec((tm, 128), row), pl.BlockSpec((tm, IDX_DIM), row),
    )
    return pl.pallas_call(
        _inproj_kernel,
        grid=(m // tm,),
        in_specs=[pl.BlockSpec((tm, D_MODEL), row), _vmem_spec()],
        out_specs=out_specs,
        out_shape=out_shape,
        compiler_params=_cparams(("parallel",)),
        name="in_proj",
    )(x, w_pad)


def _s5_kernel(u_ref, h0re_ref, h0im_ref, are_ref, aim_ref, bblk_ref, cblk_ref, d_ref, wglu_ref,
               out_ref, hre_ref, him_ref, hs_ref, st_ref, y_ref, *, rows_per_step, steps):
    c = pl.program_id(0)
    rps = rows_per_step
    n_rows = rps * steps

    @pl.when(c == 0)
    def _():
        st_ref[0] = h0re_ref[...]
        st_ref[1] = h0im_ref[...]

    ub = u_ref[...].astype(BF16)
    for j in range(N_SCHUNK):
        hs_ref[j] = _dot(ub[:, UCHUNK * j:UCHUNK * (j + 1)], bblk_ref[j])

    for j in range(N_SCHUNK):
        cols = slice(SCHUNK * j, SCHUNK * (j + 1))
        a_re = jnp.broadcast_to(are_ref[:, cols], (8, SCHUNK))
        a_im = jnp.broadcast_to(aim_ref[:, cols], (8, SCHUNK))
        for rg in range(rps // 8):
            rsl = slice(8 * rg, 8 * (rg + 1))

            def step(t, carry, j=j, rg=rg, a_re=a_re, a_im=a_im):
                h_re, h_im = carry
                row = pl.multiple_of(t * rps + 8 * rg, 8)
                n_re = a_re * h_re - a_im * h_im + hs_ref[j, pl.ds(row, 8), 0:SCHUNK]
                n_im = a_re * h_im + a_im * h_re + hs_ref[j, pl.ds(row, 8), SCHUNK:2 * SCHUNK]
                hs_ref[j, pl.ds(row, 8), 0:SCHUNK] = n_re
                hs_ref[j, pl.ds(row, 8), SCHUNK:2 * SCHUNK] = n_im
                return n_re, n_im

            h_re, h_im = lax.fori_loop(0, steps, step, (st_ref[0, rsl, cols], st_ref[1, rsl, cols]),
                                       unroll=min(steps, 8))
            st_ref[0, rsl, cols] = h_re
            st_ref[1, rsl, cols] = h_im

    rc = 256
    for r0 in range(0, n_rows, rc):
        for j in range(N_SCHUNK):
            y_ref[r0:r0 + rc, UCHUNK * j:UCHUNK * (j + 1)] = _dot(
                hs_ref[j, r0:r0 + rc, :].astype(BF16), cblk_ref[j])
    y = y_ref[...] + d_ref[...] * u_ref[...]
    g = jax.nn.gelu(y).astype(BF16)
    z = _dot(g, wglu_ref[...])
    out_ref[...] = (z[:, :D_SSM] * jax.nn.sigmoid(z[:, D_SSM:])).astype(out_ref.dtype)

    @pl.when(c == pl.num_programs(0) - 1)
    def _():
        hre_ref[...] = st_ref[0]
        him_ref[...] = st_ref[1]


def _s5_call(u_t, h0_re, h0_im, a_re, a_im, bblk, cblk, d, wglu, rows_per_step, steps):
    n_rows_total = u_t.shape[0]
    n_rows = rows_per_step * steps
    kern = functools.partial(_s5_kernel, rows_per_step=rows_per_step, steps=steps)
    return pl.pallas_call(
        kern,
        grid=(n_rows_total // n_rows,),
        in_specs=[pl.BlockSpec((n_rows, D_SSM), lambda c: (c, 0))] + [_vmem_spec()] * 8,
        out_specs=(pl.BlockSpec((n_rows, D_SSM), lambda c: (c, 0)),
                   pl.BlockSpec((rows_per_step, N_STATE), lambda c: (0, 0)),
                   pl.BlockSpec((rows_per_step, N_STATE), lambda c: (0, 0))),
        out_shape=(jax.ShapeDtypeStruct((n_rows_total, D_SSM), BF16),
                   jax.ShapeDtypeStruct((rows_per_step, N_STATE), F32),
                   jax.ShapeDtypeStruct((rows_per_step, N_STATE), F32)),
        scratch_shapes=[pltpu.VMEM((N_SCHUNK, n_rows, 2 * SCHUNK), F32),
                        pltpu.VMEM((2, rows_per_step, N_STATE), F32),
                        pltpu.VMEM((n_rows, D_SSM), F32)],
        compiler_params=_cparams(("arbitrary",)),
        name="s5_mixer",
    )(u_t, h0_re, h0_im, a_re, a_im, bblk, cblk, d, wglu)


def _sortable_key(score):
    bits = pltpu.bitcast(score + 0.0, I32)
    return jnp.where(bits < 0, bits ^ 0x7FFFFFFF, bits)


TQ = 256
TK = 256


def _attn_prompt_kernel(qiT_ref, wT_ref, kib_ref, qT_ref, k_ref, vT_ref, bias_ref, o_ref,
                        keys_ref, s_ref, acc_ref, p_ref, *, topk):
    i = pl.program_id(1)
    n_kt = i + 1
    sub = lax.broadcasted_iota(I32, (TK, TQ), 0)
    lane = lax.broadcasted_iota(I32, (TK, TQ), 1)
    causal_diag = sub <= lane

    def score_body(j, carry):
        kt = kib_ref[pl.ds(pl.multiple_of(j * TK, TK), TK), :]
        acc = jnp.zeros((TK, TQ), F32)
        for h in range(N_IDX_HEADS):
            s = _dot(kt, qiT_ref[IDX_DIM * h:IDX_DIM * (h + 1), :])
            acc = acc + jnp.maximum(s, 0.0) * wT_ref[h:h + 1, :]
        key = _sortable_key(acc * IDX_SCALE)
        ok = jnp.logical_or(j < i, causal_diag)
        keys_ref[j] = jnp.where(ok, key, INT_MIN)
        return carry

    lax.fori_loop(0, n_kt, score_body, 0)

    def count(pred_fn):
        def body(j, c):
            hit = pred_fn(j, keys_ref[j]).astype(I32)
            return c + jnp.sum(hit.reshape(TK // 8, 8, TQ), axis=0)
        c8 = lax.fori_loop(0, n_kt, body, jnp.zeros((8, TQ), I32))
        return jnp.sum(c8, axis=0, keepdims=True)

    zero = jnp.zeros((1, TQ), I32)
    thr = jnp.where(count(lambda j, kk: kk >= zero) >= topk, zero, jnp.full((1, TQ), INT_MIN, I32))

    def bit_body(bi, thr):
        cand = thr | jnp.left_shift(jnp.int32(1), 30 - bi)
        return jnp.where(count(lambda j, kk: kk >= cand) >= topk, cand, thr)

    thr = lax.fori_loop(0, 31, bit_body, thr)

    cnt_ge = count(lambda j, kk: kk >= thr)
    p_ref[...] = jnp.full((1, TQ), 2 ** 30, I32)

    @pl.when(jnp.max(cnt_ge) > topk)
    def _():
        need = topk - count(lambda j, kk: kk > thr)

        def pos_body(bi, pos):
            cand = pos | jnp.left_shift(jnp.int32(1), 10 - bi)
            before = count(lambda j, kk: jnp.logical_and(kk == thr, sub + j * TK < cand))
            return jnp.where(before < need, cand, pos)

        p_ref[...] = lax.fori_loop(0, 11, pos_body, zero)

    pos = p_ref[...]
    acc_ref[...] = jnp.zeros(acc_ref.shape, F32)

    def logits_body(j, mx):
        kk = keys_ref[j]
        tie_ok = jnp.logical_and(kk == thr, sub + j * TK <= pos)
        sel = jnp.logical_and(jnp.logical_or(kk > thr, tie_ok), kk != INT_MIN)
        neg = jnp.where(sel, 0.0, NEG_BIG)
        kind = jnp.minimum(i - j, 2)
        row0 = pl.multiple_of(j * TK, TK)
        new = []
        for h in range(N_HEADS):
            g = h // N_REP
            s = _dot(k_ref[g, pl.ds(row0, TK), :], qT_ref[HEAD_DIM * h:HEAD_DIM * (h + 1), :])
            s = s + bias_ref[h, kind] + neg
            s_ref[h, j] = s
            new.append(jnp.maximum(mx[h], jnp.max(s.reshape(TK // 8, 8, TQ), axis=0)))
        return tuple(new)

    mx = lax.fori_loop(0, n_kt, logits_body, tuple(jnp.full((8, TQ), NEG_BIG, F32) for _ in range(N_HEADS)))
    mx = tuple(jnp.broadcast_to(jnp.max(m, axis=0, keepdims=True), (8, TQ)) for m in mx)

    def pv_body(j, ls):
        new = []
        for h in range(N_HEADS):
            g = h // N_REP
            hs = slice(HEAD_DIM * h, HEAD_DIM * (h + 1))
            p = jnp.exp(s_ref[h, j].reshape(TK // 8, 8, TQ) - mx[h][None])
            new.append(ls[h] + jnp.sum(p, axis=0))
            pv = _dot(vT_ref[j, HEAD_DIM * g:HEAD_DIM * (g + 1), :], p.reshape(TK, TQ).astype(BF16))
            acc_ref[hs, :] = acc_ref[hs, :] + pv
        return tuple(new)

    ls = lax.fori_loop(0, n_kt, pv_body, tuple(jnp.zeros((8, TQ), F32) for _ in range(N_HEADS)))

    for h in range(N_HEADS):
        hs = slice(HEAD_DIM * h, HEAD_DIM * (h + 1))
        acc_ref[hs, :] = acc_ref[hs, :] / jnp.sum(ls[h], axis=0, keepdims=True)
    o_ref[...] = acc_ref[...].T.astype(o_ref.dtype)


def _attn_prompt_call(qiT, wT, kib, qT, k2, vT, biasT, bsz, seq, topk):
    nq = seq // TQ
    nkt = seq // TK
    kern = functools.partial(_attn_prompt_kernel, topk=topk)
    return pl.pallas_call(
        kern,
        grid=(bsz, nq),
        in_specs=[
            pl.BlockSpec((512, TQ), lambda b, i: (0, b * nq + i)),
            pl.BlockSpec((N_IDX_HEADS, TQ), lambda b, i: (0, b * nq + i)),
            pl.BlockSpec((seq, IDX_DIM), lambda b, i: (b, 0)),
            pl.BlockSpec((512, TQ), lambda b, i: (0, b * nq + i)),
            pl.BlockSpec((N_KV_HEADS, seq, HEAD_DIM), lambda b, i: (0, b, 0)),
            pl.BlockSpec((None, nkt, 128, TK), lambda b, i: (b, 0, 0, 0)),
            _vmem_spec(),
        ],
        out_specs=pl.BlockSpec((TQ, D_ATTN), lambda b, i: (b * nq + i, 0)),
        out_shape=jax.ShapeDtypeStruct((bsz * seq, D_ATTN), BF16),
        scratch_shapes=[pltpu.VMEM((nkt, TK, TQ), I32),
                        pltpu.VMEM((N_HEADS, nkt, TK, TQ), F32),
                        pltpu.VMEM((D_ATTN, TQ), F32),
                        pltpu.VMEM((1, TQ), I32)],
        compiler_params=_cparams(("parallel", "arbitrary")),
        name="attn_prompt",
    )(qiT, wT, kib, qT, k2, vT, biasT)


NB = 8
N_PAGES = 16
PAST = N_PAGES * PAGE_SIZE
S_ALL = PAST + PAGE_SIZE
T_DEC = 4
QROWS = T_DEC * N_HEADS
_NT = (((1,), (1,)), ((), ()))


def _kidx_copies(pt_ref, ckiT_hbm, kiT_buf, sem, layer, step, slot):
    copies = []
    for n in range(NB):
        for pg in range(N_PAGES):
            page = pt_ref[step * NB + n, pg]
            dst = kiT_buf.at[slot, n, :, pl.ds(pg * PAGE_SIZE, PAGE_SIZE)]
            copies.append(pltpu.make_async_copy(ckiT_hbm.at[layer, page], dst, sem.at[slot]))
    return copies


def _kv_copies(pt_ref, ckT_hbm, cvT_hbm, kT_buf, vT_buf, sem, layer, b, kslot):
    copies = []
    for pg in range(N_PAGES):
        page = pt_ref[b, pg]
        cols = pl.ds(pg * PAGE_SIZE, PAGE_SIZE)
        copies.append(pltpu.make_async_copy(ckT_hbm.at[layer, page], kT_buf.at[kslot, :, cols], sem.at[0, kslot]))
        copies.append(pltpu.make_async_copy(cvT_hbm.at[layer, page], vT_buf.at[kslot, :, cols], sem.at[1, kslot]))
    return copies


def _attn_sample_kernel(pt_ref, qi_ref, w_ref, q_ref, kinT_ref, knT_ref, vnT_ref, bias_ref,
                        ckT_hbm, cvT_hbm, ckiT_hbm, o_ref,
                        kT_buf, vT_buf, kiT_buf, keys_ref, sel_ref, prob_ref, sem_ki, sem_kv, *, layer, topk):
    s = pl.program_id(0)
    n_steps = pl.num_programs(0)
    slot = lax.rem(s, 2)
    b0 = s * NB
    ki_fetch = functools.partial(_kidx_copies, pt_ref, ckiT_hbm, kiT_buf, sem_ki, layer)
    kv_fetch = functools.partial(_kv_copies, pt_ref, ckT_hbm, cvT_hbm, kT_buf, vT_buf, sem_kv, layer)

    @pl.when(s == 0)
    def _():
        for cp in ki_fetch(0, 0):
            cp.start()

    for cp in kv_fetch(b0, 0):
        cp.start()

    @pl.when(s + 1 < n_steps)
    def _():
        for cp in ki_fetch(s + 1, 1 - slot):
            cp.start()

    for cp in ki_fetch(s, slot):
        cp.wait()

    lane = lax.broadcasted_iota(I32, (1, S_ALL), 1)

    for n in range(NB):
        qi = qi_ref[n]
        sc = jnp.concatenate([_dot(qi, kiT_buf[slot, n].astype(BF16)), _dot(qi, kinT_ref[n])], axis=1)
        sc = jnp.maximum(sc, 0.0) * w_ref[n]
        for t in range(T_DEC):
            row = jnp.sum(sc[N_IDX_HEADS * t:N_IDX_HEADS * (t + 1), :], axis=0, keepdims=True) * IDX_SCALE
            key = jnp.where(lane <= PAST + t, _sortable_key(row), INT_MIN)
            keys_ref[T_DEC * n + t:T_DEC * n + t + 1, :] = key

    n_grp = NB * T_DEC // 8
    grp = [slice(8 * g, 8 * (g + 1)) for g in range(n_grp)]

    def count(hit):
        return jnp.sum(hit.astype(I32), axis=1, keepdims=True)

    zero = jnp.zeros((8, 1), I32)
    thr = tuple(jnp.where(count(keys_ref[gs, :] >= zero) >= topk, zero, jnp.full((8, 1), INT_MIN, I32))
                for gs in grp)

    def bit_body(bi, thr):
        bit = jnp.left_shift(jnp.int32(1), 30 - bi)
        out = []
        for g, gs in enumerate(grp):
            cand = thr[g] | bit
            out.append(jnp.where(count(keys_ref[gs, :] >= cand) >= topk, cand, thr[g]))
        return tuple(out)

    thr = lax.fori_loop(0, 31, bit_body, thr)
    need = tuple(topk - count(keys_ref[gs, :] > thr[g]) for g, gs in enumerate(grp))
    lane8 = lax.broadcasted_iota(I32, (8, S_ALL), 1)

    def pos_body(bi, pos):
        bit = jnp.left_shift(jnp.int32(1), 11 - bi)
        out = []
        for g, gs in enumerate(grp):
            cand = pos[g] | bit
            before = count(jnp.logical_and(keys_ref[gs, :] == thr[g], lane8 < cand))
            out.append(jnp.where(before < need[g], cand, pos[g]))
        return tuple(out)

    pos = lax.fori_loop(0, 12, pos_body, tuple(zero for _ in grp))
    for g, gs in enumerate(grp):
        kk = keys_ref[gs, :]
        tie_ok = jnp.logical_and(kk == thr[g], lane8 <= pos[g])
        sel = jnp.logical_and(jnp.logical_or(kk > thr[g], tie_ok), kk != INT_MIN).astype(I32)
        sel_ref[2 * g] = sel[0:T_DEC]
        sel_ref[2 * g + 1] = sel[T_DEC:2 * T_DEC]

    def attend_body(n, carry):
        kslot = lax.rem(n, 2)

        @pl.when(n + 1 < NB)
        def _():
            for cp in kv_fetch(b0 + n + 1, 1 - kslot):
                cp.start()

        for cp in kv_fetch(b0 + n, kslot):
            cp.wait()

        q = q_ref[n]
        lg = jnp.concatenate([_dot(q, kT_buf[kslot].astype(BF16)), _dot(q, knT_ref[n])], axis=1) + bias_ref[...]
        sel_n = sel_ref[n]
        for t in range(T_DEC):
            rs = slice(N_HEADS * t, N_HEADS * (t + 1))
            x = jnp.where(sel_n[t:t + 1, :] > 0, lg[rs, :], NEG_BIG)
            mx = jnp.max(x, axis=1, keepdims=True)
            e = jnp.exp(x - mx)
            prob_ref[rs, :] = e / jnp.sum(e, axis=1, keepdims=True)
        p = prob_ref[...].astype(BF16)
        o_ref[n] = (lax.dot_general(p[:, :PAST], vT_buf[kslot].astype(BF16), _NT, preferred_element_type=F32)
                    + lax.dot_general(p[:, PAST:], vnT_ref[n], _NT, preferred_element_type=F32))
        return carry

    lax.fori_loop(0, NB, attend_body, 0)


def _attn_sample_call(page_table, qi_s, w_s, q_s, kinT, knT, vnT, bias_s, ckT, cvT, ckiT, layer, topk):
    bsz = qi_s.shape[0]
    kern = functools.partial(_attn_sample_kernel, layer=layer, topk=topk)
    blk3 = lambda shp: pl.BlockSpec((NB,) + shp, lambda s, pt: (s, 0, 0))
    any_spec = pl.BlockSpec(memory_space=pl.ANY)
    grid_spec = pltpu.PrefetchScalarGridSpec(
        num_scalar_prefetch=1,
        grid=(bsz // NB,),
        in_specs=[blk3((QROWS, IDX_DIM)), blk3((QROWS, 1)), blk3((QROWS, 128)),
                  blk3((IDX_DIM, PAGE_SIZE)), blk3((128, PAGE_SIZE)), blk3((128, PAGE_SIZE)),
                  pl.BlockSpec((QROWS, S_ALL), lambda s, pt: (0, 0)),
                  any_spec, any_spec, any_spec],
        out_specs=blk3((QROWS, 128)),
        scratch_shapes=[pltpu.VMEM((2, 128, PAST), F32),
                        pltpu.VMEM((2, 128, PAST), F32),
                        pltpu.VMEM((2, NB, IDX_DIM, PAST), F32),
                        pltpu.VMEM((NB * T_DEC, S_ALL), I32),
                        pltpu.VMEM((NB, T_DEC, S_ALL), I32),
                        pltpu.VMEM((QROWS, S_ALL), F32),
                        pltpu.SemaphoreType.DMA((2,)),
                        pltpu.SemaphoreType.DMA((2, 2))],
    )
    return pl.pallas_call(
        kern,
        grid_spec=grid_spec,
        out_shape=jax.ShapeDtypeStruct((bsz, QROWS, 128), F32),
        compiler_params=_cparams(("arbitrary",)),
        name="attn_sample",
    )(page_table, qi_s, w_s, q_s, kinT, knT, vnT, bias_s, ckT, cvT, ckiT)


def _outproj_kernel(x_ref, ssm_ref, attn_ref, wt_ref, wb_ref, g_ref, b_ref, o_ref):
    mixed = _dot(ssm_ref[...], wt_ref[...]) + _dot(attn_ref[...], wb_ref[...])
    o_ref[...] = _layer_norm(ALPHA * x_ref[...] + mixed, g_ref[...], b_ref[...])


def _outproj_call(x, ssm, attn, w_top, w_bot, g, b, tm, ssm_tmajor_tiles):
    m = x.shape[0]
    row = lambda i: (i, 0)
    if ssm_tmajor_tiles:
        n = ssm_tmajor_tiles
        ssm_spec = pl.BlockSpec((tm, D_SSM), lambda i: (i % n, i // n))
    else:
        ssm_spec = pl.BlockSpec((tm, D_SSM), row)
    return pl.pallas_call(
        _outproj_kernel,
        grid=(m // tm,),
        in_specs=[pl.BlockSpec((tm, D_MODEL), row), ssm_spec, pl.BlockSpec((tm, D_ATTN), row),
                  _vmem_spec(), _vmem_spec(), _vmem_spec(), _vmem_spec()],
        out_specs=pl.BlockSpec((tm, D_MODEL), row),
        out_shape=jax.ShapeDtypeStruct((m, D_MODEL), F32),
        compiler_params=_cparams(("parallel",)),
        name="out_proj_ln1",
    )(x, ssm, attn, w_top, w_bot, g, b)


def _ffn_kernel(x_ref, wg_ref, wu_ref, wd_ref, o_ref, xb_ref, acc_ref):
    f = pl.program_id(1)

    @pl.when(f == 0)
    def _():
        xb_ref[...] = x_ref[...].astype(BF16)
        acc_ref[...] = jnp.zeros(acc_ref.shape, F32)

    xb = xb_ref[...]
    h = jax.nn.silu(_dot(xb, wg_ref[...])) * _dot(xb, wu_ref[...])
    acc_ref[...] += _dot(h.astype(BF16), wd_ref[...])

    @pl.when(f == pl.num_programs(1) - 1)
    def _():
        o_ref[...] = acc_ref[...]


def _ffn_call(x, wg, wu, wd, tm, tf):
    m = x.shape[0]
    dff = wg.shape[1]
    return pl.pallas_call(
        _ffn_kernel,
        grid=(m // tm, dff // tf),
        in_specs=[pl.BlockSpec((tm, D_MODEL), lambda i, f: (i, 0)),
                  pl.BlockSpec((D_MODEL, tf), lambda i, f: (0, f)),
                  pl.BlockSpec((D_MODEL, tf), lambda i, f: (0, f)),
                  pl.BlockSpec((tf, D_MODEL), lambda i, f: (f, 0))],
        out_specs=pl.BlockSpec((tm, D_MODEL), lambda i, f: (i, 0)),
        out_shape=jax.ShapeDtypeStruct((m, D_MODEL), F32),
        scratch_shapes=[pltpu.VMEM((tm, D_MODEL), BF16), pltpu.VMEM((tm, D_MODEL), F32)],
        compiler_params=_cparams(("parallel", "arbitrary")),
        name="ffn_swiglu",
    )(x, wg, wu, wd)


def _router_kernel(x_ref, wr_ref, br_ref, g_ref):
    logits = jnp.dot(x_ref[...], wr_ref[...], preferred_element_type=F32,
                     precision=lax.Precision.HIGHEST) + br_ref[...]
    lane = lax.broadcasted_iota(I32, logits.shape, 1)
    logits = jnp.where(lane < N_EXPERTS, logits, -jnp.inf)
    lane_f = lane.astype(F32)
    m1 = jnp.max(logits, axis=1, keepdims=True)
    i1 = jnp.min(jnp.where(logits == m1, lane_f, 128.0), axis=1, keepdims=True)
    rest = jnp.where(lane_f == i1, -jnp.inf, logits)
    m2 = jnp.max(rest, axis=1, keepdims=True)
    i2 = jnp.min(jnp.where(rest == m2, lane_f, 128.0), axis=1, keepdims=True)
    e2 = jnp.exp(m2 - m1)
    den = 1.0 + e2
    g_ref[...] = jnp.where(lane_f == i1, 1.0 / den, 0.0) + jnp.where(lane_f == i2, e2 / den, 0.0)


def _router_call(x, wr_pad, br_pad, tm):
    m = x.shape[0]
    return pl.pallas_call(
        _router_kernel,
        grid=(m // tm,),
        in_specs=[pl.BlockSpec((tm, D_MODEL), lambda i: (i, 0)), _vmem_spec(), _vmem_spec()],
        out_specs=pl.BlockSpec((tm, 128), lambda i: (i, 0)),
        out_shape=jax.ShapeDtypeStruct((m, 128), F32),
        compiler_params=_cparams(("parallel",)),
        name="moe_router",
    )(x, wr_pad, br_pad)


def _moe_kernel(x_ref, gate_ref, wg_ref, wu_ref, wd_ref, o_ref, xb_ref, acc_ref):
    e = pl.program_id(1)

    @pl.when(e == 0)
    def _():
        xb_ref[...] = x_ref[...].astype(BF16)
        acc_ref[...] = jnp.zeros(acc_ref.shape, F32)

    xb = xb_ref[...]
    h = jax.nn.silu(_dot(xb, wg_ref[...])) * _dot(xb, wu_ref[...])
    y = _dot(h.astype(BF16), wd_ref[...])
    gates = gate_ref[...]
    lane = lax.broadcasted_iota(I32, gates.shape, 1)
    ge = jnp.sum(jnp.where(lane == e, gates, 0.0), axis=1, keepdims=True)
    acc_ref[...] += ge * y

    @pl.when(e == pl.num_programs(1) - 1)
    def _():
        o_ref[...] = acc_ref[...]


def _moe_call(x, gates, wg, wu, wd, tm):
    m = x.shape[0]
    return pl.pallas_call(
        _moe_kernel,
        grid=(m // tm, N_EXPERTS),
        in_specs=[pl.BlockSpec((tm, D_MODEL), lambda i, e: (i, 0)),
                  pl.BlockSpec((tm, 128), lambda i, e: (i, 0)),
                  pl.BlockSpec((None, D_MODEL, D_FF_EXPERT), lambda i, e: (e, 0, 0)),
                  pl.BlockSpec((None, D_MODEL, D_FF_EXPERT), lambda i, e: (e, 0, 0)),
                  pl.BlockSpec((None, D_FF_EXPERT, D_MODEL), lambda i, e: (e, 0, 0))],
        out_specs=pl.BlockSpec((tm, D_MODEL), lambda i, e: (i, 0)),
        out_shape=jax.ShapeDtypeStruct((m, D_MODEL), F32),
        scratch_shapes=[pltpu.VMEM((tm, D_MODEL), BF16), pltpu.VMEM((tm, D_MODEL), F32)],
        compiler_params=_cparams(("parallel", "arbitrary")),
        name="moe_swiglu",
    )(x, gates, wg, wu, wd)


RT = 256
SEG_ALIGN = 16
TG = 512


def _route_kernel(x_ref, wr_ref, br_ref, g_ref, lr_ref, lrT_ref, cnt_ref):
    logits = jnp.dot(x_ref[...], wr_ref[...], preferred_element_type=F32,
                     precision=lax.Precision.HIGHEST) + br_ref[...]
    lane = lax.broadcasted_iota(I32, logits.shape, 1)
    logits = jnp.where(lane < N_EXPERTS, logits, -jnp.inf)
    lane_f = lane.astype(F32)
    m1 = jnp.max(logits, axis=1, keepdims=True)
    i1 = jnp.min(jnp.where(logits == m1, lane_f, 128.0), axis=1, keepdims=True)
    rest = jnp.where(lane_f == i1, -jnp.inf, logits)
    m2 = jnp.max(rest, axis=1, keepdims=True)
    i2 = jnp.min(jnp.where(rest == m2, lane_f, 128.0), axis=1, keepdims=True)
    e2 = jnp.exp(m2 - m1)
    den = 1.0 + e2
    g_ref[...] = jnp.where(lane_f == i1, 1.0 / den, 0.0) + jnp.where(lane_f == i2, e2 / den, 0.0)
    chosen = jnp.logical_or(lane_f == i1, lane_f == i2)
    chosen_f = jnp.where(chosen, 1.0, 0.0)
    r = lax.broadcasted_iota(I32, (RT, RT), 0)
    c = lax.broadcasted_iota(I32, (RT, RT), 1)
    before = jnp.where(r > c, 1.0, 0.0).astype(BF16)
    rank = _dot(before, chosen_f.astype(BF16))
    lr = jnp.where(chosen, rank, -1.0)
    lr_ref[...] = lr
    lrT_ref[...] = lr.T[0:N_EXPERTS, :]
    cnt_ref[...] = jnp.broadcast_to(jnp.sum(chosen_f, axis=0, keepdims=True), (8, 128))


def _route_call(x, wr_pad, br_pad):
    m = x.shape[0]
    nt = m // RT
    return pl.pallas_call(
        _route_kernel,
        grid=(nt,),
        in_specs=[pl.BlockSpec((RT, D_MODEL), lambda i: (i, 0)), _vmem_spec(), _vmem_spec()],
        out_specs=(pl.BlockSpec((RT, 128), lambda i: (i, 0)),
                   pl.BlockSpec((RT, 128), lambda i: (i, 0)),
                   pl.BlockSpec((N_EXPERTS, RT), lambda i: (0, i)),
                   pl.BlockSpec((8, 128), lambda i: (i, 0))),
        out_shape=(jax.ShapeDtypeStruct((m, 128), F32),
                   jax.ShapeDtypeStruct((m, 128), F32),
                   jax.ShapeDtypeStruct((N_EXPERTS, m), F32),
                   jax.ShapeDtypeStruct((nt * 8, 128), F32)),
        compiler_params=_cparams(("parallel",)),
        name="moe_route",
    )(x, wr_pad, br_pad)


def _segment_plan(counts, nt, n_ffn_tiles):
    cnt = counts.reshape(nt, 8, 128)[:, 0, :N_EXPERTS].astype(I32)
    r = (cnt + SEG_ALIGN - 1) // SEG_ALIGN * SEG_ALIGN
    used = jnp.sum(r, axis=0)
    region = (used + RT + TG - 1) // TG * TG
    ends = jnp.cumsum(region)
    off = ends - region
    seg = off[None, :] + jnp.cumsum(r, axis=0) - r
    g0 = jnp.arange(n_ffn_tiles, dtype=I32) * TG
    te = jnp.minimum(jnp.sum((g0[:, None] >= ends[None, :]).astype(I32), axis=1), N_EXPERTS - 1)
    valid = jnp.logical_and(g0 < ends[-1], g0 - off[te] < used[te])
    return seg.reshape(-1), te, valid.astype(I32)


def _dispatch_copies(seg_ref, buf, xs_hbm, sem, t, slot):
    copies = []
    for e in range(N_EXPERTS):
        start = pl.multiple_of(seg_ref[t * N_EXPERTS + e], SEG_ALIGN)
        copies.append(pltpu.make_async_copy(buf.at[slot, e], xs_hbm.at[pl.ds(start, RT), :], sem.at[slot]))
    return copies


def _dispatch_kernel(seg_ref, x_ref, lrT_ref, xs0_hbm, xs_hbm, buf, sem):
    del xs0_hbm
    t = pl.program_id(0)
    slot = lax.rem(t, 2)
    xb = x_ref[...].astype(BF16)
    slot_row = lax.broadcasted_iota(I32, (RT, RT), 0).astype(F32)
    for e in range(N_EXPERTS):
        pick = jnp.where(lrT_ref[e:e + 1, :] == slot_row, 1.0, 0.0).astype(BF16)
        buf[slot, e] = _dot(pick, xb).astype(BF16)

    @pl.when(t > 0)
    def _():
        for cp in _dispatch_copies(seg_ref, buf, xs_hbm, sem, t - 1, 1 - slot):
            cp.wait()

    for cp in _dispatch_copies(seg_ref, buf, xs_hbm, sem, t, slot):
        cp.start()

    @pl.when(t == pl.num_programs(0) - 1)
    def _():
        for cp in _dispatch_copies(seg_ref, buf, xs_hbm, sem, t, slot):
            cp.wait()


def _dispatch_call(seg, x, lrT, n_slots):
    m = x.shape[0]
    xs0 = jnp.zeros((n_slots, D_MODEL), BF16)
    grid_spec = pltpu.PrefetchScalarGridSpec(
        num_scalar_prefetch=1,
        grid=(m // RT,),
        in_specs=[pl.BlockSpec((RT, D_MODEL), lambda t, seg: (t, 0)),
                  pl.BlockSpec((N_EXPERTS, RT), lambda t, seg: (0, t)),
                  pl.BlockSpec(memory_space=pl.ANY)],
        out_specs=pl.BlockSpec(memory_space=pl.ANY),
        scratch_shapes=[pltpu.VMEM((2, N_EXPERTS, RT, D_MODEL), BF16), pltpu.SemaphoreType.DMA((2,))],
    )
    return pl.pallas_call(
        _dispatch_kernel,
        grid_spec=grid_spec,
        out_shape=jax.ShapeDtypeStruct((n_slots, D_MODEL), BF16),
        input_output_aliases={3: 0},
        compiler_params=_cparams(("arbitrary",)),
        name="moe_dispatch",
    )(seg, x, lrT, xs0)


def _expert_ffn_kernel(te_ref, valid_ref, x_ref, wg_ref, wu_ref, wd_ref, o_ref):
    g = pl.program_id(0)

    @pl.when(valid_ref[g] > 0)
    def _():
        xb = x_ref[...]
        h = jax.nn.silu(_dot(xb, wg_ref[...])) * _dot(xb, wu_ref[...])
        o_ref[...] = _dot(h.astype(BF16), wd_ref[...])

    @pl.when(valid_ref[g] == 0)
    def _():
        o_ref[...] = jnp.zeros(o_ref.shape, F32)


def _expert_ffn_call(te, valid, xs, wg, wu, wd):
    n_slots = xs.shape[0]
    wmap = lambda g, te, valid: (te[g], 0, 0)
    grid_spec = pltpu.PrefetchScalarGridSpec(
        num_scalar_prefetch=2,
        grid=(n_slots // TG,),
        in_specs=[pl.BlockSpec((TG, D_MODEL), lambda g, te, valid: (g, 0)),
                  pl.BlockSpec((None, D_MODEL, D_FF_EXPERT), wmap),
                  pl.BlockSpec((None, D_MODEL, D_FF_EXPERT), wmap),
                  pl.BlockSpec((None, D_FF_EXPERT, D_MODEL), wmap)],
        out_specs=pl.BlockSpec((TG, D_MODEL), lambda g, te, valid: (g, 0)),
    )
    return pl.pallas_call(
        _expert_ffn_kernel,
        grid_spec=grid_spec,
        out_shape=jax.ShapeDtypeStruct((n_slots, D_MODEL), F32),
        compiler_params=_cparams(("arbitrary",)),
        name="moe_expert_ffn",
    )(te, valid, xs, wg, wu, wd)


def _combine_copies(seg_ref, ys_hbm, buf, sem, t, slot):
    copies = []
    for e in range(N_EXPERTS):
        start = pl.multiple_of(seg_ref[t * N_EXPERTS + e], SEG_ALIGN)
        copies.append(pltpu.make_async_copy(ys_hbm.at[pl.ds(start, RT), :], buf.at[slot, e], sem.at[slot]))
    return copies


def _combine_kernel(seg_ref, gate_ref, lr_ref, ys_hbm, o_ref, buf, sem):
    t = pl.program_id(0)
    slot = lax.rem(t, 2)

    @pl.when(t == 0)
    def _():
        for cp in _combine_copies(seg_ref, ys_hbm, buf, sem, 0, 0):
            cp.start()

    @pl.when(t + 1 < pl.num_programs(0))
    def _():
        for cp in _combine_copies(seg_ref, ys_hbm, buf, sem, t + 1, 1 - slot):
            cp.start()

    for cp in _combine_copies(seg_ref, ys_hbm, buf, sem, t, slot):
        cp.wait()

    slot_col = lax.broadcasted_iota(I32, (RT, RT), 1).astype(F32)
    acc = jnp.zeros((RT, D_MODEL), F32)
    for e in range(N_EXPERTS):
        pick = jnp.where(lr_ref[:, e:e + 1] == slot_col, 1.0, 0.0).astype(BF16)
        y = buf[slot, e]
        hi = y.astype(BF16)
        lo = (y - hi.astype(F32)).astype(BF16)
        acc = acc + gate_ref[:, e:e + 1] * (_dot(pick, hi) + _dot(pick, lo))
    o_ref[...] = acc


def _combine_call(seg, gates, lr, ys):
    m = gates.shape[0]
    grid_spec = pltpu.PrefetchScalarGridSpec(
        num_scalar_prefetch=1,
        grid=(m // RT,),
        in_specs=[pl.BlockSpec((RT, 128), lambda t, seg: (t, 0)),
                  pl.BlockSpec((RT, 128), lambda t, seg: (t, 0)),
                  pl.BlockSpec(memory_space=pl.ANY)],
        out_specs=pl.BlockSpec((RT, D_MODEL), lambda t, seg: (t, 0)),
        scratch_shapes=[pltpu.VMEM((2, N_EXPERTS, RT, D_MODEL), F32), pltpu.SemaphoreType.DMA((2,))],
    )
    return pl.pallas_call(
        _combine_kernel,
        grid_spec=grid_spec,
        out_shape=jax.ShapeDtypeStruct((m, D_MODEL), F32),
        compiler_params=_cparams(("arbitrary",)),
        name="moe_combine",
    )(seg, gates, lr, ys)


def _routed_moe(x, wr_pad, br_pad, wg, wu, wd):
    m = x.shape[0]
    nt = m // RT
    max_rows = 2 * m + nt * N_EXPERTS * (SEG_ALIGN - 1) + N_EXPERTS * (RT + TG - 1)
    n_ffn_tiles = -(-max_rows // TG)
    gates, lr, lrT, counts = _route_call(x, wr_pad, br_pad)
    seg, te, valid = _segment_plan(counts, nt, n_ffn_tiles)
    xs = _dispatch_call(seg, x, lrT, n_ffn_tiles * TG)
    ys = _expert_ffn_call(te, valid, xs, wg, wu, wd)
    return _combine_call(seg, gates, lr, ys)


def _ple_ln2_kernel(x_ref, cm_ref, p_ref, wpg_ref, wpp_ref, g_ref, b_ref, o_ref):
    x = x_ref[...]
    gate = jax.nn.sigmoid(_dot(x.astype(BF16), wpg_ref[...]))
    ple = gate * _dot(p_ref[...].astype(BF16), wpp_ref[...])
    o_ref[...] = _layer_norm(ALPHA * x + cm_ref[...] + ple, g_ref[...], b_ref[...])


def _ple_ln2_call(x, cm, p_all, layer, wpg, wpp, g, b, tm):
    m = x.shape[0]
    row = lambda i: (i, 0)
    return pl.pallas_call(
        _ple_ln2_kernel,
        grid=(m // tm,),
        in_specs=[pl.BlockSpec((tm, D_MODEL), row), pl.BlockSpec((tm, D_MODEL), row),
                  pl.BlockSpec((None, tm, D_PLE), lambda i: (layer, i, 0)),
                  _vmem_spec(), _vmem_spec(), _vmem_spec(), _vmem_spec()],
        out_specs=pl.BlockSpec((tm, D_MODEL), row),
        out_shape=jax.ShapeDtypeStruct((m, D_MODEL), F32),
        compiler_params=_cparams(("parallel",)),
        name="ple_ln2",
    )(x, cm, p_all, wpg, wpp, g, b)


def _t5_bucket(dist):
    max_exact = N_BUCKETS // 2
    n = jnp.maximum(dist, 0)
    nf = jnp.maximum(n, 1).astype(F32)
    large = max_exact + (jnp.log(nf / max_exact) / math.log(MAX_DISTANCE / max_exact)
                         * (N_BUCKETS - max_exact)).astype(I32)
    large = jnp.minimum(large, N_BUCKETS - 1)
    return jnp.where(n < max_exact, n, large)


def _ssm_matrices(a_re, a_im, log_dt, b_re, b_im, c_re, c_im):
    lam = lax.complex(a_re, a_im)
    dt = jnp.exp(log_dt)[:, None]
    a_bar = jnp.exp(lam * dt)
    b_bar = ((a_bar - 1.0) / lam)[:, :, None] * lax.complex(b_re, b_im)
    gpc = N_SSM_GROUPS // N_SCHUNK
    eye = jnp.eye(gpc, dtype=F32)

    def in_block(x):
        x = x.reshape(N_SCHUNK, gpc, SSM_STATE, SSM_GROUP)
        blk = jnp.einsum('jgnc,gh->jgchn', x, eye)
        return blk.reshape(N_SCHUNK, gpc * SSM_GROUP, gpc * SSM_STATE)

    def out_block(x):
        x = x.reshape(N_SCHUNK, gpc, SSM_GROUP, SSM_STATE)
        blk = jnp.einsum('jgcn,gh->jgnhc', x, eye)
        return blk.reshape(N_SCHUNK, gpc * SSM_STATE, gpc * SSM_GROUP)

    bblk = jnp.concatenate([in_block(b_bar.real), in_block(b_bar.imag)], axis=2).astype(BF16)
    cblk = jnp.concatenate([out_block(c_re), out_block(-c_im)], axis=1).astype(BF16)
    return (a_bar.real.reshape(1, N_STATE), a_bar.imag.reshape(1, N_STATE), bblk, cblk)


def _bias_lookup(rel_bias, dist):
    bucket = _t5_bucket(jnp.asarray(dist))[None]
    out = jnp.zeros((N_HEADS,) + dist.shape, F32)
    for b in range(N_BUCKETS):
        out = jnp.where(bucket == b, rel_bias[b].reshape((N_HEADS,) + (1,) * dist.ndim), out)
    return out


def _prompt_bias_tiles(rel_bias):
    c = np.arange(TK)[:, None]
    r = np.arange(TQ)[None, :]
    dist = np.stack([kind * TQ + r - c for kind in range(3)]).astype(np.int32)
    return _bias_lookup(rel_bias, dist)


def _sample_bias_rows(rel_bias):
    key_pos = np.arange(S_ALL)[None, :]
    q_pos = PAST + np.arange(T_DEC)[:, None]
    dist = (q_pos - key_pos).astype(np.int32)
    return jnp.transpose(_bias_lookup(rel_bias, dist), (1, 0, 2)).reshape(QROWS, S_ALL)


def kernel(x_prompt, x_sample, p_prompt, p_sample, cache_k, cache_v, cache_kidx, state_ssm_re, state_ssm_im, page_table, ln_emb_g, ln_emb_b, w_in, ssm_a_re, ssm_a_im, ssm_log_dt, ssm_b_re, ssm_b_im, ssm_c_re, ssm_c_im, ssm_d, ssm_w_glu, rel_bias, w_out, ln1_g, ln1_b, ffn_w_gate, ffn_w_up, ffn_w_down, moe_w_router, moe_b_router, moe_w_gate, moe_w_up, moe_w_down, ple_w_gate, ple_w_proj, ln2_g, ln2_b):
    bsz, seq, _ = x_prompt.shape
    dbsz, dseq, _ = x_sample.shape
    assert dseq == T_DEC and page_table.shape[1] == N_PAGES and dbsz % NB == 0
    mp = bsz * seq
    ms = dbsz * dseq
    tm = 512
    tiles_per_seq = seq // tm
    topk_p = min(TOPK_MAX, seq // 4)
    topk_s = min(TOPK_MAX, (PAST + dseq) // 4)
    n_pool = cache_k.shape[1]

    row2 = lambda a: a.reshape(1, -1)
    xp = _ln_call(x_prompt.reshape(mp, D_MODEL), row2(ln_emb_g), row2(ln_emb_b), tm)
    xs = _ln_call(x_sample.reshape(ms, D_MODEL), row2(ln_emb_g), row2(ln_emb_b), tm)

    ckT = jnp.transpose(cache_k, (0, 1, 3, 4, 2)).reshape(DEPTH, n_pool, 128, PAGE_SIZE)
    cvT = jnp.transpose(cache_v, (0, 1, 3, 4, 2)).reshape(DEPTH, n_pool, 128, PAGE_SIZE)
    ckiT = jnp.transpose(cache_kidx, (0, 1, 3, 2))
    bias_p = _prompt_bias_tiles(rel_bias)
    bias_s = _sample_bias_rows(rel_bias)
    h0_prompt = jnp.zeros((bsz, N_STATE), F32)

    outs = [[] for _ in range(10)]
    for i in range(DEPTH):
        w_pad = jnp.pad(w_in[i], ((0, 0), (0, D_IN_PAD - D_IN))).astype(BF16)
        a_re, a_im, bblk, cblk = _ssm_matrices(ssm_a_re[i], ssm_a_im[i], ssm_log_dt[i], ssm_b_re[i], ssm_b_im[i],
                                               ssm_c_re[i], ssm_c_im[i])
        d_row = row2(ssm_d[i])
        wglu = ssm_w_glu[i].astype(BF16)
        w_top = w_out[i, :D_SSM].astype(BF16)
        w_bot = w_out[i, D_SSM:].astype(BF16)
        wpg = ple_w_gate[i].astype(BF16)
        wpp = ple_w_proj[i].astype(BF16)
        j = i // 2

        def channel_mixer(x1, tm_cm, routed):
            if i % 2 == 0:
                return _ffn_call(x1, ffn_w_gate[j].astype(BF16), ffn_w_up[j].astype(BF16),
                                 ffn_w_down[j].astype(BF16), tm_cm, D_FF // 2)
            wr_pad = jnp.pad(moe_w_router[j], ((0, 0), (0, 128 - N_EXPERTS)))
            br_pad = jnp.pad(moe_b_router[j], (0, 128 - N_EXPERTS)).reshape(1, 128)
            wg, wu, wd = moe_w_gate[j].astype(BF16), moe_w_up[j].astype(BF16), moe_w_down[j].astype(BF16)
            if routed:
                return _routed_moe(x1, wr_pad, br_pad, wg, wu, wd)
            gates = _router_call(x1, wr_pad, br_pad, 512)
            return _moe_call(x1, gates, wg, wu, wd, tm_cm)

        u_t, q, k, v, qi, tail, kb, vb, kib = _inproj_call(xp, w_pad, tm, tiles_per_seq)
        ssm_t, hre, him = _s5_call(u_t.reshape(seq * bsz, D_SSM), h0_prompt, h0_prompt, a_re, a_im, bblk, cblk,
                                   d_row, wglu, rows_per_step=bsz, steps=128)
        k2 = jnp.transpose(kb.reshape(mp, N_KV_HEADS, HEAD_DIM), (1, 0, 2))
        vT = jnp.transpose(vb.reshape(bsz, seq // TK, TK, 128), (0, 1, 3, 2))
        attn = _attn_prompt_call(qi.T, tail[:, IDX_DIM:IDX_DIM + N_IDX_HEADS].T, kib, q.T, k2, vT, bias_p,
                                 bsz, seq, topk_p)
        x1 = _outproj_call(xp, ssm_t.reshape(seq, bsz * D_SSM), attn, w_top, w_bot, row2(ln1_g[i]), row2(ln1_b[i]),
                           tm, tiles_per_seq)
        cm = channel_mixer(x1, 512, True)
        xp = _ple_ln2_call(x1, cm, p_prompt.reshape(DEPTH, mp, D_PLE), i, wpg, wpp, row2(ln2_g[i]), row2(ln2_b[i]), tm)
        outs[0].append(k.reshape(bsz, seq, N_KV_HEADS, HEAD_DIM))
        outs[1].append(v.reshape(bsz, seq, N_KV_HEADS, HEAD_DIM))
        outs[2].append(tail[:, :IDX_DIM].reshape(bsz, seq, IDX_DIM))
        outs[3].append(hre.reshape(bsz, N_SSM_GROUPS, SSM_STATE))
        outs[4].append(him.reshape(bsz, N_SSM_GROUPS, SSM_STATE))

        u, q, k, v, qi, tail, kb, vb, kib = _inproj_call(xs, w_pad, tm, 0)
        u_t = jnp.transpose(u.reshape(dbsz, dseq, D_SSM), (1, 0, 2)).reshape(ms, D_SSM)
        ssm_t, hre, him = _s5_call(u_t, state_ssm_re[i].reshape(dbsz, N_STATE), state_ssm_im[i].reshape(dbsz, N_STATE),
                                   a_re, a_im, bblk, cblk, d_row, wglu, rows_per_step=dbsz, steps=dseq)
        ssm = jnp.transpose(ssm_t.reshape(dseq, dbsz, D_SSM), (1, 0, 2)).reshape(ms, D_SSM)
        q4 = q.reshape(dbsz, dseq, N_KV_HEADS, N_REP, HEAD_DIM)
        q_exp = jnp.einsum('btgrd,gh->btgrhd', q4, jnp.eye(N_KV_HEADS, dtype=BF16)).reshape(dbsz, QROWS, 128)
        pad_new = lambda a: jnp.transpose(
            jnp.pad(a.reshape(dbsz, dseq, -1), ((0, 0), (0, PAGE_SIZE - dseq), (0, 0))), (0, 2, 1))
        o_s = _attn_sample_call(page_table, qi.reshape(dbsz, QROWS, IDX_DIM),
                                tail[:, IDX_DIM:IDX_DIM + N_IDX_HEADS].reshape(dbsz, QROWS, 1), q_exp,
                                pad_new(kib), pad_new(kb), pad_new(vb), bias_s,
                                ckT, cvT, ckiT, i, topk_s)
        o5 = o_s.reshape(dbsz, dseq, N_KV_HEADS, N_REP, N_KV_HEADS, HEAD_DIM)
        attn = jnp.einsum('btgrhd,gh->btgrd', o5, jnp.eye(N_KV_HEADS, dtype=F32)).reshape(ms, D_ATTN).astype(BF16)
        x1 = _outproj_call(xs, ssm, attn, w_top, w_bot, row2(ln1_g[i]), row2(ln1_b[i]), tm, 0)
        cm = channel_mixer(x1, 512, False)
        xs = _ple_ln2_call(x1, cm, p_sample.reshape(DEPTH, ms, D_PLE), i, wpg, wpp, row2(ln2_g[i]), row2(ln2_b[i]), tm)
        outs[5].append(k.reshape(dbsz, dseq, N_KV_HEADS, HEAD_DIM))
        outs[6].append(v.reshape(dbsz, dseq, N_KV_HEADS, HEAD_DIM))
        outs[7].append(tail[:, :IDX_DIM].reshape(dbsz, dseq, IDX_DIM))
        outs[8].append(hre.reshape(dbsz, N_SSM_GROUPS, SSM_STATE))
        outs[9].append(him.reshape(dbsz, N_SSM_GROUPS, SSM_STATE))

    return (xp.reshape(bsz, seq, D_MODEL), xs.reshape(dbsz, dseq, D_MODEL)) + tuple(jnp.stack(o) for o in outs)
```

```python
import functools
import math

import numpy as np
import jax
import jax.numpy as jnp
from jax import lax
from jax.experimental import pallas as pl
from jax.experimental.pallas import tpu as pltpu

F32 = jnp.float32
BF16 = jnp.bfloat16
I32 = jnp.int32

D_MODEL = 1024
DEPTH = 4
PAGE_SIZE = 128
D_SSM = 512
D_ATTN = 512
SSM_GROUP = 16
N_SSM_GROUPS = 32
SSM_STATE = 64
HEAD_DIM = 64
N_HEADS = 8
N_KV_HEADS = 2
N_REP = 4
N_IDX_HEADS = 8
IDX_DIM = 64
IDX_SCALE = (IDX_DIM ** -0.5) * (N_IDX_HEADS ** -0.5)
TOPK_MAX = 256
N_BUCKETS = 32
MAX_DISTANCE = 128
D_FF = 2816
N_EXPERTS = 8
D_FF_EXPERT = 1408
D_PLE = 256
LN_EPS = 1e-5
ALPHA = (2 * DEPTH) ** 0.25
D_IN = 1864
D_IN_PAD = 1920
N_STATE = N_SSM_GROUPS * SSM_STATE
N_SCHUNK = 4
SCHUNK = N_STATE // N_SCHUNK
UCHUNK = D_SSM // N_SCHUNK

INT_MIN = -2 ** 31
NEG_BIG = -1e30
VMEM_LIMIT = 56 * 1024 * 1024


def _cparams(sem):
    return pltpu.CompilerParams(dimension_semantics=sem, vmem_limit_bytes=VMEM_LIMIT)


def _vmem_spec():
    return pl.BlockSpec(memory_space=pltpu.VMEM)


def _layer_norm(x, g, b):
    mu = jnp.mean(x, axis=-1, keepdims=True)
    xc = x - mu
    var = jnp.mean(xc * xc, axis=-1, keepdims=True)
    return xc * lax.rsqrt(var + LN_EPS) * g + b


def _dot(a, b):
    return jnp.dot(a, b, preferred_element_type=F32)


def _ln_kernel(x_ref, g_ref, b_ref, o_ref):
    o_ref[...] = _layer_norm(x_ref[...], g_ref[...], b_ref[...])


def _ln_call(x, g, b, tm):
    m = x.shape[0]
    return pl.pallas_call(
        _ln_kernel,
        grid=(m // tm,),
        in_specs=[pl.BlockSpec((tm, D_MODEL), lambda i: (i, 0)), _vmem_spec(), _vmem_spec()],
        out_specs=pl.BlockSpec((tm, D_MODEL), lambda i: (i, 0)),
        out_shape=jax.ShapeDtypeStruct((m, D_MODEL), F32),
        compiler_params=_cparams(("parallel",)),
        name="ln_embed",
    )(x, g, b)


def _inproj_kernel(x_ref, w_ref, u_ref, q_ref, k_ref, v_ref, qi_ref, tail_ref, kb_ref, vb_ref, kib_ref):
    xb = x_ref[...].astype(BF16)
    u_ref[...] = _dot(xb, w_ref[:, 0:512])
    q_ref[...] = (_dot(xb, w_ref[:, 512:1024]) * (HEAD_DIM ** -0.5)).astype(BF16)
    k = _dot(xb, w_ref[:, 1024:1152])
    v = _dot(xb, w_ref[:, 1152:1280])
    k_ref[...] = k
    v_ref[...] = v
    kb_ref[...] = k.astype(BF16)
    vb_ref[...] = v.astype(BF16)
    qi_ref[...] = _dot(xb, w_ref[:, 1280:1792]).astype(BF16)
    tail = _dot(xb, w_ref[:, 1792:1920])
    tail_ref[...] = tail
    kib_ref[...] = tail[:, 0:IDX_DIM].astype(BF16)


def _inproj_prompt_kernel(x_ref, w_ref, u_ref, qT_ref, k_ref, v_ref, qiT_ref, tail_ref, wT_ref, k2_ref, vT_ref,
                          kib_ref):
    xb = x_ref[...].astype(BF16)
    u_ref[...] = _dot(xb, w_ref[:, 0:512])
    qT_ref[...] = (_dot(xb, w_ref[:, 512:1024]) * (HEAD_DIM ** -0.5)).T.astype(BF16)
    k = _dot(xb, w_ref[:, 1024:1152])
    v = _dot(xb, w_ref[:, 1152:1280])
    k_ref[...] = k
    v_ref[...] = v
    kb = k.astype(BF16)
    for g in range(N_KV_HEADS):
        k2_ref[g] = kb[:, HEAD_DIM * g:HEAD_DIM * (g + 1)]
    for c in range(vT_ref.shape[0]):
        vT_ref[c] = v[TK * c:TK * (c + 1), :].T.astype(BF16)
    qiT_ref[...] = _dot(xb, w_ref[:, 1280:1792]).T.astype(BF16)
    tail = _dot(xb, w_ref[:, 1792:1920])
    tail_ref[...] = tail
    wT_ref[...] = tail.T[IDX_DIM:IDX_DIM + N_IDX_HEADS, :]
    kib_ref[...] = tail[:, 0:IDX_DIM].astype(BF16)


def _inproj_prompt_call(x, w_all, layer, tm, bsz, seq):
    m = x.shape[0]
    n = seq // tm
    kt = tm // TK
    row = lambda i: (i, 0)
    col = lambda i: (0, i)
    out_shape = (
        jax.ShapeDtypeStruct((seq, bsz * D_SSM), F32),
        jax.ShapeDtypeStruct((512, m), BF16),
        jax.ShapeDtypeStruct((m, 128), F32),
        jax.ShapeDtypeStruct((m, 128), F32),
        jax.ShapeDtypeStruct((512, m), BF16),
        jax.ShapeDtypeStruct((m, 128), F32),
        jax.ShapeDtypeStruct((N_IDX_HEADS, m), F32),
        jax.ShapeDtypeStruct((N_KV_HEADS, m, HEAD_DIM), BF16),
        jax.ShapeDtypeStruct((bsz, seq // TK, 128, TK), BF16),
        jax.ShapeDtypeStruct((m, IDX_DIM), BF16),
    )
    out_specs = (
        pl.BlockSpec((tm, D_SSM), lambda i: (i % n, i // n)),
        pl.BlockSpec((512, tm), col), pl.BlockSpec((tm, 128), row), pl.BlockSpec((tm, 128), row),
        pl.BlockSpec((512, tm), col), pl.BlockSpec((tm, 128), row), pl.BlockSpec((N_IDX_HEADS, tm), col),
        pl.BlockSpec((N_KV_HEADS, tm, HEAD_DIM), lambda i: (0, i, 0)),
        pl.BlockSpec((None, kt, 128, TK), lambda i: (i // n, i % n, 0, 0)),
        pl.BlockSpec((tm, IDX_DIM), row),
    )
    return pl.pallas_call(
        _inproj_prompt_kernel,
        grid=(m // tm,),
        in_specs=[pl.BlockSpec((tm, D_MODEL), row),
                  pl.BlockSpec((None, D_MODEL, D_IN_PAD), lambda i: (layer, 0, 0))],
        out_specs=out_specs,
        out_shape=out_shape,
        compiler_params=_cparams(("parallel",)),
        name="in_proj_prompt",
    )(x, w_all)


def _inproj_call(x, w_all, layer, tm):
    m = x.shape[0]
    row = lambda i: (i, 0)
    out_shape = (
        jax.ShapeDtypeStruct((m, D_SSM), F32),
        jax.ShapeDtypeStruct((m, 512), BF16),
        jax.ShapeDtypeStruct((m, 128), F32),
        jax.ShapeDtypeStruct((m, 128), F32),
        jax.ShapeDtypeStruct((m, 512), BF16),
        jax.ShapeDtypeStruct((m, 128), F32),
        jax.ShapeDtypeStruct((m, 128), BF16),
        jax.ShapeDtypeStruct((m, 128), BF16),
        jax.ShapeDtypeStruct((m, IDX_DIM), BF16),
    )
    out_specs = (
        pl.BlockSpec((tm, D_SSM), row),
        pl.BlockSpec((tm, 512), row), pl.BlockSpec((tm, 128), row), pl.BlockSpec((tm, 128), row),
        pl.BlockSpec((tm, 512), row), pl.BlockSpec((tm, 128), row), pl.BlockSpec((tm, 128), row),
        pl.BlockSpec((tm, 128), row), pl.BlockSpec((tm, IDX_DIM), row),
    )
    return pl.pallas_call(
        _inproj_kernel,
        grid=(m // tm,),
        in_specs=[pl.BlockSpec((tm, D_MODEL), row),
                  pl.BlockSpec((None, D_MODEL, D_IN_PAD), lambda i: (layer, 0, 0))],
        out_specs=out_specs,
        out_shape=out_shape,
        compiler_params=_cparams(("parallel",)),
        name="in_proj",
    )(x, w_all)


def _s5_kernel(u_ref, h0re_ref, h0im_ref, are_ref, aim_ref, bblk_ref, cblk_ref, d_ref, wglu_ref,
               out_ref, hre_ref, him_ref, hs_ref, st_ref, y_ref, *, rows_per_step, steps):
    c = pl.program_id(0)
    rps = rows_per_step
    n_rows = rps * steps

    @pl.when(c == 0)
    def _():
        st_ref[0] = h0re_ref[...]
        st_ref[1] = h0im_ref[...]

    ub = u_ref[...].astype(BF16)
    for j in range(N_SCHUNK):
        hs_ref[j] = _dot(ub[:, UCHUNK * j:UCHUNK * (j + 1)], bblk_ref[j])

    for j in range(N_SCHUNK):
        cols = slice(SCHUNK * j, SCHUNK * (j + 1))
        a_re = jnp.broadcast_to(are_ref[:, cols], (8, SCHUNK))
        a_im = jnp.broadcast_to(aim_ref[:, cols], (8, SCHUNK))
        for rg in range(rps // 8):
            rsl = slice(8 * rg, 8 * (rg + 1))

            def step(t, carry, j=j, rg=rg, a_re=a_re, a_im=a_im):
                h_re, h_im = carry
                row = pl.multiple_of(t * rps + 8 * rg, 8)
                n_re = a_re * h_re - a_im * h_im + hs_ref[j, pl.ds(row, 8), 0:SCHUNK]
                n_im = a_re * h_im + a_im * h_re + hs_ref[j, pl.ds(row, 8), SCHUNK:2 * SCHUNK]
                hs_ref[j, pl.ds(row, 8), 0:SCHUNK] = n_re
                hs_ref[j, pl.ds(row, 8), SCHUNK:2 * SCHUNK] = n_im
                return n_re, n_im

            h_re, h_im = lax.fori_loop(0, steps, step, (st_ref[0, rsl, cols], st_ref[1, rsl, cols]),
                                       unroll=min(steps, 8))
            st_ref[0, rsl, cols] = h_re
            st_ref[1, rsl, cols] = h_im

    rc = 256
    for r0 in range(0, n_rows, rc):
        for j in range(N_SCHUNK):
            y_ref[r0:r0 + rc, UCHUNK * j:UCHUNK * (j + 1)] = _dot(
                hs_ref[j, r0:r0 + rc, :].astype(BF16), cblk_ref[j])
    y = y_ref[...] + d_ref[...] * u_ref[...]
    g = jax.nn.gelu(y).astype(BF16)
    z = _dot(g, wglu_ref[...])
    out_ref[...] = (z[:, :D_SSM] * jax.nn.sigmoid(z[:, D_SSM:])).astype(out_ref.dtype)

    @pl.when(c == pl.num_programs(0) - 1)
    def _():
        hre_ref[...] = st_ref[0]
        him_ref[...] = st_ref[1]


def _s5_call(u_t, h0_re, h0_im, a_re, a_im, bblk, cblk, d, wglu, layer, rows_per_step, steps):
    n_rows_total = u_t.shape[0]
    n_rows = rows_per_step * steps
    kern = functools.partial(_s5_kernel, rows_per_step=rows_per_step, steps=steps)

    def layer_spec(a):
        nd = a.ndim - 1
        return pl.BlockSpec((None,) + a.shape[1:], lambda c: (layer,) + (0,) * nd)

    return pl.pallas_call(
        kern,
        grid=(n_rows_total // n_rows,),
        in_specs=[pl.BlockSpec((n_rows, D_SSM), lambda c: (c, 0)), _vmem_spec(), _vmem_spec()]
        + [layer_spec(a) for a in (a_re, a_im, bblk, cblk, d, wglu)],
        out_specs=(pl.BlockSpec((n_rows, D_SSM), lambda c: (c, 0)),
                   pl.BlockSpec((rows_per_step, N_STATE), lambda c: (0, 0)),
                   pl.BlockSpec((rows_per_step, N_STATE), lambda c: (0, 0))),
        out_shape=(jax.ShapeDtypeStruct((n_rows_total, D_SSM), BF16),
                   jax.ShapeDtypeStruct((rows_per_step, N_STATE), F32),
                   jax.ShapeDtypeStruct((rows_per_step, N_STATE), F32)),
        scratch_shapes=[pltpu.VMEM((N_SCHUNK, n_rows, 2 * SCHUNK), F32),
                        pltpu.VMEM((2, rows_per_step, N_STATE), F32),
                        pltpu.VMEM((n_rows, D_SSM), F32)],
        compiler_params=_cparams(("arbitrary",)),
        name="s5_mixer",
    )(u_t, h0_re, h0_im, a_re, a_im, bblk, cblk, d, wglu)


def _sortable_key(score):
    bits = pltpu.bitcast(score + 0.0, I32)
    return jnp.where(bits < 0, bits ^ 0x7FFFFFFF, bits)


TQ = 256
TK = 256


def _attn_prompt_kernel(qiT_ref, wT_ref, kib_ref, qT_ref, k_ref, vT_ref, bias_ref, o_ref,
                        keys_ref, s_ref, acc_ref, p_ref, *, topk):
    i = pl.program_id(1)
    n_kt = i + 1
    sub = lax.broadcasted_iota(I32, (TK, TQ), 0)
    lane = lax.broadcasted_iota(I32, (TK, TQ), 1)
    causal_diag = sub <= lane

    def score_body(j, carry):
        kt = kib_ref[pl.ds(pl.multiple_of(j * TK, TK), TK), :]
        acc = jnp.zeros((TK, TQ), F32)
        for h in range(N_IDX_HEADS):
            s = _dot(kt, qiT_ref[IDX_DIM * h:IDX_DIM * (h + 1), :])
            acc = acc + jnp.maximum(s, 0.0) * wT_ref[h:h + 1, :]
        key = _sortable_key(acc * IDX_SCALE)
        ok = jnp.logical_or(j < i, causal_diag)
        keys_ref[j] = jnp.where(ok, key, INT_MIN)
        return carry

    lax.fori_loop(0, n_kt, score_body, 0)

    def count(pred_fn):
        def body(j, c):
            hit = pred_fn(j, keys_ref[j]).astype(I32)
            return c + jnp.sum(hit.reshape(TK // 8, 8, TQ), axis=0)
        c8 = lax.fori_loop(0, n_kt, body, jnp.zeros((8, TQ), I32))
        return jnp.sum(c8, axis=0, keepdims=True)

    zero = jnp.zeros((1, TQ), I32)
    thr = jnp.where(count(lambda j, kk: kk >= zero) >= topk, zero, jnp.full((1, TQ), INT_MIN, I32))

    def bit_body(bi, thr):
        cand = thr | jnp.left_shift(jnp.int32(1), 30 - bi)
        return jnp.where(count(lambda j, kk: kk >= cand) >= topk, cand, thr)

    thr = lax.fori_loop(0, 31, bit_body, thr)

    cnt_ge = count(lambda j, kk: kk >= thr)
    p_ref[...] = jnp.full((1, TQ), 2 ** 30, I32)

    @pl.when(jnp.max(cnt_ge) > topk)
    def _():
        need = topk - count(lambda j, kk: kk > thr)

        def pos_body(bi, pos):
            cand = pos | jnp.left_shift(jnp.int32(1), 10 - bi)
            before = count(lambda j, kk: jnp.logical_and(kk == thr, sub + j * TK < cand))
            return jnp.where(before < need, cand, pos)

        p_ref[...] = lax.fori_loop(0, 11, pos_body, zero)

    pos = p_ref[...]
    acc_ref[...] = jnp.zeros(acc_ref.shape, F32)

    def logits_body(j, mx):
        kk = keys_ref[j]
        tie_ok = jnp.logical_and(kk == thr, sub + j * TK <= pos)
        sel = jnp.logical_and(jnp.logical_or(kk > thr, tie_ok), kk != INT_MIN)
        neg = jnp.where(sel, 0.0, NEG_BIG)
        kind = jnp.minimum(i - j, 2)
        row0 = pl.multiple_of(j * TK, TK)
        new = []
        for h in range(N_HEADS):
            g = h // N_REP
            s = _dot(k_ref[g, pl.ds(row0, TK), :], qT_ref[HEAD_DIM * h:HEAD_DIM * (h + 1), :])
            s = s + bias_ref[h, kind] + neg
            s_ref[h, j] = s
            new.append(jnp.maximum(mx[h], jnp.max(s.reshape(TK // 8, 8, TQ), axis=0)))
        return tuple(new)

    mx = lax.fori_loop(0, n_kt, logits_body, tuple(jnp.full((8, TQ), NEG_BIG, F32) for _ in range(N_HEADS)))
    mx = tuple(jnp.broadcast_to(jnp.max(m, axis=0, keepdims=True), (8, TQ)) for m in mx)

    def pv_body(j, ls):
        new = []
        for h in range(N_HEADS):
            g = h // N_REP
            hs = slice(HEAD_DIM * h, HEAD_DIM * (h + 1))
            p = jnp.exp(s_ref[h, j].reshape(TK // 8, 8, TQ) - mx[h][None])
            new.append(ls[h] + jnp.sum(p, axis=0))
            pv = _dot(vT_ref[j, HEAD_DIM * g:HEAD_DIM * (g + 1), :], p.reshape(TK, TQ).astype(BF16))
            acc_ref[hs, :] = acc_ref[hs, :] + pv
        return tuple(new)

    ls = lax.fori_loop(0, n_kt, pv_body, tuple(jnp.zeros((8, TQ), F32) for _ in range(N_HEADS)))

    for h in range(N_HEADS):
        hs = slice(HEAD_DIM * h, HEAD_DIM * (h + 1))
        acc_ref[hs, :] = acc_ref[hs, :] / jnp.sum(ls[h], axis=0, keepdims=True)
    o_ref[...] = acc_ref[...].T.astype(o_ref.dtype)


def _attn_prompt_call(qiT, wT, kib, qT, k2, vT, biasT, bsz, seq, topk):
    nq = seq // TQ
    nkt = seq // TK
    kern = functools.partial(_attn_prompt_kernel, topk=topk)
    return pl.pallas_call(
        kern,
        grid=(bsz, nq),
        in_specs=[
            pl.BlockSpec((512, TQ), lambda b, i: (0, b * nq + i)),
            pl.BlockSpec((N_IDX_HEADS, TQ), lambda b, i: (0, b * nq + i)),
            pl.BlockSpec((seq, IDX_DIM), lambda b, i: (b, 0)),
            pl.BlockSpec((512, TQ), lambda b, i: (0, b * nq + i)),
            pl.BlockSpec((N_KV_HEADS, seq, HEAD_DIM), lambda b, i: (0, b, 0)),
            pl.BlockSpec((None, nkt, 128, TK), lambda b, i: (b, 0, 0, 0)),
            _vmem_spec(),
        ],
        out_specs=pl.BlockSpec((TQ, D_ATTN), lambda b, i: (b * nq + i, 0)),
        out_shape=jax.ShapeDtypeStruct((bsz * seq, D_ATTN), BF16),
        scratch_shapes=[pltpu.VMEM((nkt, TK, TQ), I32),
                        pltpu.VMEM((N_HEADS, nkt, TK, TQ), F32),
                        pltpu.VMEM((D_ATTN, TQ), F32),
                        pltpu.VMEM((1, TQ), I32)],
        compiler_params=_cparams(("parallel", "arbitrary")),
        name="attn_prompt",
    )(qiT, wT, kib, qT, k2, vT, biasT)


NB = 8
N_PAGES = 16
PAST = N_PAGES * PAGE_SIZE
S_ALL = PAST + PAGE_SIZE
T_DEC = 4
QROWS = T_DEC * N_HEADS
_NT = (((1,), (1,)), ((), ()))


def _kidx_copies(pt_ref, ckiT_hbm, kiT_buf, sem, layer, step, slot):
    copies = []
    for n in range(NB):
        for pg in range(N_PAGES):
            page = pt_ref[step * NB + n, pg]
            dst = kiT_buf.at[slot, n, :, pl.ds(pg * PAGE_SIZE, PAGE_SIZE)]
            copies.append(pltpu.make_async_copy(ckiT_hbm.at[layer, page], dst, sem.at[slot]))
    return copies


def _kv_copies(pt_ref, ckT_hbm, cvT_hbm, kT_buf, vT_buf, sem, layer, b, kslot):
    copies = []
    for pg in range(N_PAGES):
        page = pt_ref[b, pg]
        cols = pl.ds(pg * PAGE_SIZE, PAGE_SIZE)
        copies.append(pltpu.make_async_copy(ckT_hbm.at[layer, page], kT_buf.at[kslot, :, cols], sem.at[0, kslot]))
        copies.append(pltpu.make_async_copy(cvT_hbm.at[layer, page], vT_buf.at[kslot, :, cols], sem.at[1, kslot]))
    return copies


def _attn_sample_kernel(pt_ref, qi_ref, w_ref, q_ref, kinT_ref, knT_ref, vnT_ref, bias_ref,
                        ckT_hbm, cvT_hbm, ckiT_hbm, o_ref,
                        kT_buf, vT_buf, kiT_buf, keys_ref, sel_ref, prob_ref, pos_ref, sem_ki, sem_kv, *, layer, topk):
    s = pl.program_id(0)
    n_steps = pl.num_programs(0)
    slot = lax.rem(s, 2)
    b0 = s * NB
    ki_fetch = functools.partial(_kidx_copies, pt_ref, ckiT_hbm, kiT_buf, sem_ki, layer)
    kv_fetch = functools.partial(_kv_copies, pt_ref, ckT_hbm, cvT_hbm, kT_buf, vT_buf, sem_kv, layer)

    @pl.when(s == 0)
    def _():
        for cp in ki_fetch(0, 0):
            cp.start()
        for n in range(NB):
            for cp in kv_fetch(n, n):
                cp.start()

    @pl.when(s + 1 < n_steps)
    def _():
        for cp in ki_fetch(s + 1, 1 - slot):
            cp.start()

    for cp in ki_fetch(s, slot):
        cp.wait()

    lane = lax.broadcasted_iota(I32, (1, S_ALL), 1)

    for n in range(NB):
        qi = qi_ref[n]
        sc = jnp.concatenate([_dot(qi, kiT_buf[slot, n].astype(BF16)), _dot(qi, kinT_ref[n])], axis=1)
        sc = jnp.maximum(sc, 0.0) * w_ref[n]
        for t in range(T_DEC):
            row = jnp.sum(sc[N_IDX_HEADS * t:N_IDX_HEADS * (t + 1), :], axis=0, keepdims=True) * IDX_SCALE
            key = jnp.where(lane <= PAST + t, _sortable_key(row), INT_MIN)
            keys_ref[T_DEC * n + t:T_DEC * n + t + 1, :] = key

    n_grp = NB * T_DEC // 8
    grp = [slice(8 * g, 8 * (g + 1)) for g in range(n_grp)]

    def count(hit):
        return jnp.sum(hit.astype(I32), axis=1, keepdims=True)

    zero = jnp.zeros((8, 1), I32)
    thr = tuple(jnp.where(count(keys_ref[gs, :] >= zero) >= topk, zero, jnp.full((8, 1), INT_MIN, I32))
                for gs in grp)

    def bit_body(bi, thr):
        bit = jnp.left_shift(jnp.int32(1), 30 - bi)
        out = []
        for g, gs in enumerate(grp):
            cand = thr[g] | bit
            out.append(jnp.where(count(keys_ref[gs, :] >= cand) >= topk, cand, thr[g]))
        return tuple(out)

    thr = lax.fori_loop(0, 31, bit_body, thr)
    lane8 = lax.broadcasted_iota(I32, (8, S_ALL), 1)
    most = count(keys_ref[grp[0], :] >= thr[0])
    for g in range(1, n_grp):
        most = jnp.maximum(most, count(keys_ref[grp[g], :] >= thr[g]))
    pos_ref[...] = jnp.full(pos_ref.shape, 2 ** 30, I32)

    @pl.when(jnp.max(most) > topk)
    def _():
        need = tuple(topk - count(keys_ref[gs, :] > thr[g]) for g, gs in enumerate(grp))

        def pos_body(bi, pos):
            bit = jnp.left_shift(jnp.int32(1), 11 - bi)
            out = []
            for g, gs in enumerate(grp):
                cand = pos[g] | bit
                before = count(jnp.logical_and(keys_ref[gs, :] == thr[g], lane8 < cand))
                out.append(jnp.where(before < need[g], cand, pos[g]))
            return tuple(out)

        pos = lax.fori_loop(0, 12, pos_body, tuple(zero for _ in grp))
        for g in range(n_grp):
            pos_ref[g] = pos[g]

    for g, gs in enumerate(grp):
        kk = keys_ref[gs, :]
        tie_ok = jnp.logical_and(kk == thr[g], lane8 <= pos_ref[g])
        sel = jnp.logical_and(jnp.logical_or(kk > thr[g], tie_ok), kk != INT_MIN).astype(I32)
        sel_ref[2 * g] = sel[0:T_DEC]
        sel_ref[2 * g + 1] = sel[T_DEC:2 * T_DEC]

    def attend_body(n, carry):
        for cp in kv_fetch(b0 + n, n):
            cp.wait()

        q = q_ref[n]
        lg = jnp.concatenate([_dot(q, kT_buf[n].astype(BF16)), _dot(q, knT_ref[n])], axis=1) + bias_ref[...]
        sel_n = sel_ref[n]
        for t in range(T_DEC):
            rs = slice(N_HEADS * t, N_HEADS * (t + 1))
            x = jnp.where(sel_n[t:t + 1, :] > 0, lg[rs, :], NEG_BIG)
            mx = jnp.max(x, axis=1, keepdims=True)
            e = jnp.exp(x - mx)
            prob_ref[rs, :] = e / jnp.sum(e, axis=1, keepdims=True)
        p = prob_ref[...].astype(BF16)
        o_ref[n] = (lax.dot_general(p[:, :PAST], vT_buf[n].astype(BF16), _NT, preferred_element_type=F32)
                    + lax.dot_general(p[:, PAST:], vnT_ref[n], _NT, preferred_element_type=F32))

        @pl.when(s + 1 < n_steps)
        def _():
            for cp in kv_fetch(b0 + NB + n, n):
                cp.start()

        return carry

    lax.fori_loop(0, NB, attend_body, 0)


def _attn_sample_call(page_table, qi_s, w_s, q_s, kinT, knT, vnT, bias_s, ckT, cvT, ckiT, layer, topk):
    bsz = qi_s.shape[0]
    kern = functools.partial(_attn_sample_kernel, layer=layer, topk=topk)
    blk3 = lambda shp: pl.BlockSpec((NB,) + shp, lambda s, pt: (s, 0, 0))
    any_spec = pl.BlockSpec(memory_space=pl.ANY)
    grid_spec = pltpu.PrefetchScalarGridSpec(
        num_scalar_prefetch=1,
        grid=(bsz // NB,),
        in_specs=[blk3((QROWS, IDX_DIM)), blk3((QROWS, 1)), blk3((QROWS, 128)),
                  blk3((IDX_DIM, PAGE_SIZE)), blk3((128, PAGE_SIZE)), blk3((128, PAGE_SIZE)),
                  pl.BlockSpec((QROWS, S_ALL), lambda s, pt: (0, 0)),
                  any_spec, any_spec, any_spec],
        out_specs=blk3((QROWS, 128)),
        scratch_shapes=[pltpu.VMEM((NB, 128, PAST), F32),
                        pltpu.VMEM((NB, 128, PAST), F32),
                        pltpu.VMEM((2, NB, IDX_DIM, PAST), F32),
                        pltpu.VMEM((NB * T_DEC, S_ALL), I32),
                        pltpu.VMEM((NB, T_DEC, S_ALL), I32),
                        pltpu.VMEM((QROWS, S_ALL), F32),
                        pltpu.VMEM((NB * T_DEC // 8, 8, 1), I32),
                        pltpu.SemaphoreType.DMA((2,)),
                        pltpu.SemaphoreType.DMA((2, NB))],
    )
    return pl.pallas_call(
        kern,
        grid_spec=grid_spec,
        out_shape=jax.ShapeDtypeStruct((bsz, QROWS, 128), F32),
        compiler_params=_cparams(("arbitrary",)),
        name="attn_sample",
    )(page_table, qi_s, w_s, q_s, kinT, knT, vnT, bias_s, ckT, cvT, ckiT)


def _outproj_kernel(x_ref, ssm_ref, attn_ref, wt_ref, wb_ref, g_ref, b_ref, o_ref):
    mixed = _dot(ssm_ref[...], wt_ref[...]) + _dot(attn_ref[...], wb_ref[...])
    o_ref[...] = _layer_norm(ALPHA * x_ref[...] + mixed, g_ref[...], b_ref[...])


def _outproj_call(x, ssm, attn, w_all, layer, g, b, tm, ssm_tmajor_tiles):
    m = x.shape[0]
    row = lambda i: (i, 0)
    if ssm_tmajor_tiles:
        n = ssm_tmajor_tiles
        ssm_spec = pl.BlockSpec((tm, D_SSM), lambda i: (i % n, i // n))
    else:
        ssm_spec = pl.BlockSpec((tm, D_SSM), row)
    return pl.pallas_call(
        _outproj_kernel,
        grid=(m // tm,),
        in_specs=[pl.BlockSpec((tm, D_MODEL), row), ssm_spec, pl.BlockSpec((tm, D_ATTN), row),
                  pl.BlockSpec((None, D_SSM, D_MODEL), lambda i: (layer, 0, 0)),
                  pl.BlockSpec((None, D_ATTN, D_MODEL), lambda i: (layer, 1, 0)),
                  _vmem_spec(), _vmem_spec()],
        out_specs=pl.BlockSpec((tm, D_MODEL), row),
        out_shape=jax.ShapeDtypeStruct((m, D_MODEL), F32),
        compiler_params=_cparams(("parallel",)),
        name="out_proj_ln1",
    )(x, ssm, attn, w_all, w_all, g, b)


def _ffn_kernel(x_ref, wg_ref, wu_ref, wd_ref, o_ref, xb_ref, acc_ref):
    f = pl.program_id(1)

    @pl.when(f == 0)
    def _():
        xb_ref[...] = x_ref[...].astype(BF16)
        acc_ref[...] = jnp.zeros(acc_ref.shape, F32)

    xb = xb_ref[...]
    h = jax.nn.silu(_dot(xb, wg_ref[...])) * _dot(xb, wu_ref[...])
    acc_ref[...] += _dot(h.astype(BF16), wd_ref[...])

    @pl.when(f == pl.num_programs(1) - 1)
    def _():
        o_ref[...] = acc_ref[...]


def _ffn_call(x, wg, wu, wd, j, tm, tf):
    m = x.shape[0]
    dff = wg.shape[2]
    return pl.pallas_call(
        _ffn_kernel,
        grid=(m // tm, dff // tf),
        in_specs=[pl.BlockSpec((tm, D_MODEL), lambda i, f: (i, 0)),
                  pl.BlockSpec((None, D_MODEL, tf), lambda i, f: (j, 0, f)),
                  pl.BlockSpec((None, D_MODEL, tf), lambda i, f: (j, 0, f)),
                  pl.BlockSpec((None, tf, D_MODEL), lambda i, f: (j, f, 0))],
        out_specs=pl.BlockSpec((tm, D_MODEL), lambda i, f: (i, 0)),
        out_shape=jax.ShapeDtypeStruct((m, D_MODEL), F32),
        scratch_shapes=[pltpu.VMEM((tm, D_MODEL), BF16), pltpu.VMEM((tm, D_MODEL), F32)],
        compiler_params=_cparams(("parallel", "arbitrary")),
        name="ffn_swiglu",
    )(x, wg, wu, wd)


def _router_kernel(x_ref, wr_ref, br_ref, g_ref):
    logits = jnp.dot(x_ref[...], wr_ref[...], preferred_element_type=F32,
                     precision=lax.Precision.HIGHEST) + br_ref[...]
    lane = lax.broadcasted_iota(I32, logits.shape, 1)
    logits = jnp.where(lane < N_EXPERTS, logits, -jnp.inf)
    lane_f = lane.astype(F32)
    m1 = jnp.max(logits, axis=1, keepdims=True)
    i1 = jnp.min(jnp.where(logits == m1, lane_f, 128.0), axis=1, keepdims=True)
    rest = jnp.where(lane_f == i1, -jnp.inf, logits)
    m2 = jnp.max(rest, axis=1, keepdims=True)
    i2 = jnp.min(jnp.where(rest == m2, lane_f, 128.0), axis=1, keepdims=True)
    e2 = jnp.exp(m2 - m1)
    den = 1.0 + e2
    g_ref[...] = jnp.where(lane_f == i1, 1.0 / den, 0.0) + jnp.where(lane_f == i2, e2 / den, 0.0)


def _router_call(x, wr_pad, br_pad, tm):
    m = x.shape[0]
    return pl.pallas_call(
        _router_kernel,
        grid=(m // tm,),
        in_specs=[pl.BlockSpec((tm, D_MODEL), lambda i: (i, 0)), _vmem_spec(), _vmem_spec()],
        out_specs=pl.BlockSpec((tm, 128), lambda i: (i, 0)),
        out_shape=jax.ShapeDtypeStruct((m, 128), F32),
        compiler_params=_cparams(("parallel",)),
        name="moe_router",
    )(x, wr_pad, br_pad)


def _moe_kernel(x_ref, gate_ref, wg_ref, wu_ref, wd_ref, o_ref, xb_ref, acc_ref):
    e = pl.program_id(1)

    @pl.when(e == 0)
    def _():
        xb_ref[...] = x_ref[...].astype(BF16)
        acc_ref[...] = jnp.zeros(acc_ref.shape, F32)

    xb = xb_ref[...]
    h = jax.nn.silu(_dot(xb, wg_ref[...])) * _dot(xb, wu_ref[...])
    y = _dot(h.astype(BF16), wd_ref[...])
    gates = gate_ref[...]
    lane = lax.broadcasted_iota(I32, gates.shape, 1)
    ge = jnp.sum(jnp.where(lane == e, gates, 0.0), axis=1, keepdims=True)
    acc_ref[...] += ge * y

    @pl.when(e == pl.num_programs(1) - 1)
    def _():
        o_ref[...] = acc_ref[...]


def _moe_call(x, gates, wg, wu, wd, j, tm):
    m = x.shape[0]
    return pl.pallas_call(
        _moe_kernel,
        grid=(m // tm, N_EXPERTS),
        in_specs=[pl.BlockSpec((tm, D_MODEL), lambda i, e: (i, 0)),
                  pl.BlockSpec((tm, 128), lambda i, e: (i, 0)),
                  pl.BlockSpec((None, None, D_MODEL, D_FF_EXPERT), lambda i, e: (j, e, 0, 0)),
                  pl.BlockSpec((None, None, D_MODEL, D_FF_EXPERT), lambda i, e: (j, e, 0, 0)),
                  pl.BlockSpec((None, None, D_FF_EXPERT, D_MODEL), lambda i, e: (j, e, 0, 0))],
        out_specs=pl.BlockSpec((tm, D_MODEL), lambda i, e: (i, 0)),
        out_shape=jax.ShapeDtypeStruct((m, D_MODEL), F32),
        scratch_shapes=[pltpu.VMEM((tm, D_MODEL), BF16), pltpu.VMEM((tm, D_MODEL), F32)],
        compiler_params=_cparams(("parallel", "arbitrary")),
        name="moe_swiglu",
    )(x, gates, wg, wu, wd)


RT = 256
RT_SMALL = 128
SEG_ALIGN = 16
TG = 512


def _route_kernel(x_ref, wr_ref, br_ref, g_ref, lr_ref, lrT_ref, cnt_ref):
    logits = jnp.dot(x_ref[...], wr_ref[...], preferred_element_type=F32,
                     precision=lax.Precision.HIGHEST) + br_ref[...]
    lane = lax.broadcasted_iota(I32, logits.shape, 1)
    logits = jnp.where(lane < N_EXPERTS, logits, -jnp.inf)
    lane_f = lane.astype(F32)
    m1 = jnp.max(logits, axis=1, keepdims=True)
    i1 = jnp.min(jnp.where(logits == m1, lane_f, 128.0), axis=1, keepdims=True)
    rest = jnp.where(lane_f == i1, -jnp.inf, logits)
    m2 = jnp.max(rest, axis=1, keepdims=True)
    i2 = jnp.min(jnp.where(rest == m2, lane_f, 128.0), axis=1, keepdims=True)
    e2 = jnp.exp(m2 - m1)
    den = 1.0 + e2
    g_ref[...] = jnp.where(lane_f == i1, 1.0 / den, 0.0) + jnp.where(lane_f == i2, e2 / den, 0.0)
    chosen = jnp.logical_or(lane_f == i1, lane_f == i2)
    chosen_f = jnp.where(chosen, 1.0, 0.0)
    r = lax.broadcasted_iota(I32, (RT, RT), 0)
    c = lax.broadcasted_iota(I32, (RT, RT), 1)
    before = jnp.where(r > c, 1.0, 0.0).astype(BF16)
    rank = _dot(before, chosen_f.astype(BF16))
    lr = jnp.where(chosen, rank, -1.0)
    lr_ref[...] = lr
    lrT_ref[...] = lr.T[0:N_EXPERTS, :]
    cnt_ref[...] = jnp.broadcast_to(jnp.sum(chosen_f, axis=0, keepdims=True), (8, 128))


def _route_call(x, wr_pad, br_pad):
    m = x.shape[0]
    nt = m // RT
    return pl.pallas_call(
        _route_kernel,
        grid=(nt,),
        in_specs=[pl.BlockSpec((RT, D_MODEL), lambda i: (i, 0)), _vmem_spec(), _vmem_spec()],
        out_specs=(pl.BlockSpec((RT, 128), lambda i: (i, 0)),
                   pl.BlockSpec((RT, 128), lambda i: (i, 0)),
                   pl.BlockSpec((N_EXPERTS, RT), lambda i: (0, i)),
                   pl.BlockSpec((8, 128), lambda i: (i, 0))),
        out_shape=(jax.ShapeDtypeStruct((m, 128), F32),
                   jax.ShapeDtypeStruct((m, 128), F32),
                   jax.ShapeDtypeStruct((N_EXPERTS, m), F32),
                   jax.ShapeDtypeStruct((nt * 8, 128), F32)),
        compiler_params=_cparams(("parallel",)),
        name="moe_route",
    )(x, wr_pad, br_pad)


def _segment_plan(counts, nt, n_ffn_tiles):
    cnt = counts.reshape(nt, 8, 128)[:, 0, :N_EXPERTS].astype(I32)
    r = (cnt + SEG_ALIGN - 1) // SEG_ALIGN * SEG_ALIGN
    used = jnp.sum(r, axis=0)
    region = (used + RT + TG - 1) // TG * TG
    ends = jnp.cumsum(region)
    off = ends - region
    seg = off[None, :] + jnp.cumsum(r, axis=0) - r
    g0 = jnp.arange(n_ffn_tiles, dtype=I32) * TG
    te = jnp.minimum(jnp.sum((g0[:, None] >= ends[None, :]).astype(I32), axis=1), N_EXPERTS - 1)
    valid = jnp.logical_and(g0 < ends[-1], g0 - off[te] < used[te])
    small = (jnp.max(cnt, axis=1) <= RT_SMALL).astype(I32)
    return seg.reshape(-1), small, te, valid.astype(I32)


def _dispatch_copies(seg_ref, buf, xs_hbm, sem, t, slot, rows):
    copies = []
    for e in range(N_EXPERTS):
        start = pl.multiple_of(seg_ref[t * N_EXPERTS + e], SEG_ALIGN)
        copies.append(pltpu.make_async_copy(buf.at[slot, e, pl.ds(0, rows)], xs_hbm.at[pl.ds(start, rows), :],
                                            sem.at[slot]))
    return copies


def _for_block_rows(small_ref, t, fn):
    @pl.when(small_ref[t] > 0)
    def _():
        fn(RT_SMALL)

    @pl.when(small_ref[t] == 0)
    def _():
        fn(RT)


def _dispatch_kernel(seg_ref, small_ref, x_ref, lrT_ref, xs0_hbm, xs_hbm, buf, sem):
    del xs0_hbm
    t = pl.program_id(0)
    slot = lax.rem(t, 2)
    copies = functools.partial(_dispatch_copies, seg_ref, buf, xs_hbm, sem)

    def fill(rows):
        xb = x_ref[...].astype(BF16)
        slot_row = lax.broadcasted_iota(I32, (rows, RT), 0).astype(F32)
        for e in range(N_EXPERTS):
            pick = jnp.where(lrT_ref[e:e + 1, :] == slot_row, 1.0, 0.0).astype(BF16)
            buf[slot, e, 0:rows] = _dot(pick, xb).astype(BF16)

    _for_block_rows(small_ref, t, fill)

    @pl.when(t > 0)
    def _():
        _for_block_rows(small_ref, t - 1, lambda rows: [cp.wait() for cp in copies(t - 1, 1 - slot, rows)])

    _for_block_rows(small_ref, t, lambda rows: [cp.start() for cp in copies(t, slot, rows)])

    @pl.when(t == pl.num_programs(0) - 1)
    def _():
        _for_block_rows(small_ref, t, lambda rows: [cp.wait() for cp in copies(t, slot, rows)])


def _dispatch_call(seg, small, x, lrT, n_slots):
    m = x.shape[0]
    xs0 = jnp.zeros((n_slots, D_MODEL), BF16)
    grid_spec = pltpu.PrefetchScalarGridSpec(
        num_scalar_prefetch=2,
        grid=(m // RT,),
        in_specs=[pl.BlockSpec((RT, D_MODEL), lambda t, seg, small: (t, 0)),
                  pl.BlockSpec((N_EXPERTS, RT), lambda t, seg, small: (0, t)),
                  pl.BlockSpec(memory_space=pl.ANY)],
        out_specs=pl.BlockSpec(memory_space=pl.ANY),
        scratch_shapes=[pltpu.VMEM((2, N_EXPERTS, RT, D_MODEL), BF16), pltpu.SemaphoreType.DMA((2,))],
    )
    return pl.pallas_call(
        _dispatch_kernel,
        grid_spec=grid_spec,
        out_shape=jax.ShapeDtypeStruct((n_slots, D_MODEL), BF16),
        input_output_aliases={4: 0},
        compiler_params=_cparams(("arbitrary",)),
        name="moe_dispatch",
    )(seg, small, x, lrT, xs0)


def _expert_ffn_kernel(te_ref, valid_ref, x_ref, wg_ref, wu_ref, wd_ref, o_ref):
    g = pl.program_id(0)

    @pl.when(valid_ref[g] > 0)
    def _():
        xb = x_ref[...]
        h = jax.nn.silu(_dot(xb, wg_ref[...])) * _dot(xb, wu_ref[...])
        o_ref[...] = _dot(h.astype(BF16), wd_ref[...])

    @pl.when(valid_ref[g] == 0)
    def _():
        o_ref[...] = jnp.zeros(o_ref.shape, F32)


def _expert_ffn_call(te, valid, xs, wg, wu, wd, j):
    n_slots = xs.shape[0]
    wmap = lambda g, te, valid: (j, te[g], 0, 0)
    grid_spec = pltpu.PrefetchScalarGridSpec(
        num_scalar_prefetch=2,
        grid=(n_slots // TG,),
        in_specs=[pl.BlockSpec((TG, D_MODEL), lambda g, te, valid: (g, 0)),
                  pl.BlockSpec((None, None, D_MODEL, D_FF_EXPERT), wmap),
                  pl.BlockSpec((None, None, D_MODEL, D_FF_EXPERT), wmap),
                  pl.BlockSpec((None, None, D_FF_EXPERT, D_MODEL), wmap)],
        out_specs=pl.BlockSpec((TG, D_MODEL), lambda g, te, valid: (g, 0)),
    )
    return pl.pallas_call(
        _expert_ffn_kernel,
        grid_spec=grid_spec,
        out_shape=jax.ShapeDtypeStruct((n_slots, D_MODEL), F32),
        compiler_params=_cparams(("arbitrary",)),
        name="moe_expert_ffn",
    )(te, valid, xs, wg, wu, wd)


def _combine_copies(seg_ref, ys_hbm, buf, sem, t, slot, rows):
    copies = []
    for e in range(N_EXPERTS):
        start = pl.multiple_of(seg_ref[t * N_EXPERTS + e], SEG_ALIGN)
        copies.append(pltpu.make_async_copy(ys_hbm.at[pl.ds(start, rows), :], buf.at[slot, e, pl.ds(0, rows)],
                                            sem.at[slot]))
    return copies


def _combine_kernel(seg_ref, small_ref, gate_ref, lr_ref, ys_hbm, o_ref, buf, sem):
    t = pl.program_id(0)
    slot = lax.rem(t, 2)
    copies = functools.partial(_combine_copies, seg_ref, ys_hbm, buf, sem)

    @pl.when(t == 0)
    def _():
        _for_block_rows(small_ref, 0, lambda rows: [cp.start() for cp in copies(0, 0, rows)])

    @pl.when(t + 1 < pl.num_programs(0))
    def _():
        _for_block_rows(small_ref, t + 1, lambda rows: [cp.start() for cp in copies(t + 1, 1 - slot, rows)])

    def gather(rows):
        for cp in copies(t, slot, rows):
            cp.wait()
        slot_col = lax.broadcasted_iota(I32, (RT, rows), 1).astype(F32)
        acc = jnp.zeros((RT, D_MODEL), F32)
        for e in range(N_EXPERTS):
            pick = jnp.where(lr_ref[:, e:e + 1] == slot_col, 1.0, 0.0).astype(BF16)
            y = buf[slot, e, 0:rows]
            hi = y.astype(BF16)
            lo = (y - hi.astype(F32)).astype(BF16)
            acc = acc + gate_ref[:, e:e + 1] * (_dot(pick, hi) + _dot(pick, lo))
        o_ref[...] = acc

    _for_block_rows(small_ref, t, gather)


def _combine_call(seg, small, gates, lr, ys):
    m = gates.shape[0]
    grid_spec = pltpu.PrefetchScalarGridSpec(
        num_scalar_prefetch=2,
        grid=(m // RT,),
        in_specs=[pl.BlockSpec((RT, 128), lambda t, seg, small: (t, 0)),
                  pl.BlockSpec((RT, 128), lambda t, seg, small: (t, 0)),
                  pl.BlockSpec(memory_space=pl.ANY)],
        out_specs=pl.BlockSpec((RT, D_MODEL), lambda t, seg, small: (t, 0)),
        scratch_shapes=[pltpu.VMEM((2, N_EXPERTS, RT, D_MODEL), F32), pltpu.SemaphoreType.DMA((2,))],
    )
    return pl.pallas_call(
        _combine_kernel,
        grid_spec=grid_spec,
        out_shape=jax.ShapeDtypeStruct((m, D_MODEL), F32),
        compiler_params=_cparams(("arbitrary",)),
        name="moe_combine",
    )(seg, small, gates, lr, ys)


def _routed_moe(x, wr_pad, br_pad, wg, wu, wd, j):
    m = x.shape[0]
    nt = m // RT
    max_rows = 2 * m + nt * N_EXPERTS * (SEG_ALIGN - 1) + N_EXPERTS * (RT + TG - 1)
    n_ffn_tiles = -(-max_rows // TG)
    gates, lr, lrT, counts = _route_call(x, wr_pad, br_pad)
    seg, small, te, valid = _segment_plan(counts, nt, n_ffn_tiles)
    xs = _dispatch_call(seg, small, x, lrT, n_ffn_tiles * TG)
    ys = _expert_ffn_call(te, valid, xs, wg, wu, wd, j)
    return _combine_call(seg, small, gates, lr, ys)


def _ple_ln2_kernel(x_ref, cm_ref, p_ref, wpg_ref, wpp_ref, g_ref, b_ref, o_ref):
    x = x_ref[...]
    gate = jax.nn.sigmoid(_dot(x.astype(BF16), wpg_ref[...]))
    ple = gate * _dot(p_ref[...].astype(BF16), wpp_ref[...])
    o_ref[...] = _layer_norm(ALPHA * x + cm_ref[...] + ple, g_ref[...], b_ref[...])


def _ple_ln2_call(x, cm, p_all, layer, wpg, wpp, g, b, tm):
    m = x.shape[0]
    row = lambda i: (i, 0)
    return pl.pallas_call(
        _ple_ln2_kernel,
        grid=(m // tm,),
        in_specs=[pl.BlockSpec((tm, D_MODEL), row), pl.BlockSpec((tm, D_MODEL), row),
                  pl.BlockSpec((None, tm, D_PLE), lambda i: (layer, i, 0)),
                  pl.BlockSpec((None, D_MODEL, D_MODEL), lambda i: (layer, 0, 0)),
                  pl.BlockSpec((None, D_PLE, D_MODEL), lambda i: (layer, 0, 0)),
                  _vmem_spec(), _vmem_spec()],
        out_specs=pl.BlockSpec((tm, D_MODEL), row),
        out_shape=jax.ShapeDtypeStruct((m, D_MODEL), F32),
        compiler_params=_cparams(("parallel",)),
        name="ple_ln2",
    )(x, cm, p_all, wpg, wpp, g, b)


def _t5_bucket(dist):
    max_exact = N_BUCKETS // 2
    n = jnp.maximum(dist, 0)
    nf = jnp.maximum(n, 1).astype(F32)
    large = max_exact + (jnp.log(nf / max_exact) / math.log(MAX_DISTANCE / max_exact)
                         * (N_BUCKETS - max_exact)).astype(I32)
    large = jnp.minimum(large, N_BUCKETS - 1)
    return jnp.where(n < max_exact, n, large)


def _ssm_matrices(a_re, a_im, log_dt, b_re, b_im, c_re, c_im):
    lam = lax.complex(a_re, a_im)
    dt = jnp.exp(log_dt)[:, :, None]
    a_bar = jnp.exp(lam * dt)
    b_bar = ((a_bar - 1.0) / lam)[..., None] * lax.complex(b_re, b_im)
    gpc = N_SSM_GROUPS // N_SCHUNK
    eye = jnp.eye(gpc, dtype=F32)

    def in_block(x):
        x = x.reshape(DEPTH, N_SCHUNK, gpc, SSM_STATE, SSM_GROUP)
        blk = jnp.einsum('ljgnc,gh->ljgchn', x, eye)
        return blk.reshape(DEPTH, N_SCHUNK, gpc * SSM_GROUP, gpc * SSM_STATE)

    def out_block(x):
        x = x.reshape(DEPTH, N_SCHUNK, gpc, SSM_GROUP, SSM_STATE)
        blk = jnp.einsum('ljgcn,gh->ljgnhc', x, eye)
        return blk.reshape(DEPTH, N_SCHUNK, gpc * SSM_STATE, gpc * SSM_GROUP)

    bblk = jnp.concatenate([in_block(b_bar.real), in_block(b_bar.imag)], axis=3).astype(BF16)
    cblk = jnp.concatenate([out_block(c_re), out_block(-c_im)], axis=2).astype(BF16)
    return (a_bar.real.reshape(DEPTH, 1, N_STATE), a_bar.imag.reshape(DEPTH, 1, N_STATE), bblk, cblk)


def _bias_lookup(rel_bias, dist):
    bucket = _t5_bucket(jnp.asarray(dist))[None]
    out = jnp.zeros((N_HEADS,) + dist.shape, F32)
    for b in range(N_BUCKETS):
        out = jnp.where(bucket == b, rel_bias[b].reshape((N_HEADS,) + (1,) * dist.ndim), out)
    return out


def _prompt_bias_tiles(rel_bias):
    c = np.arange(TK)[:, None]
    r = np.arange(TQ)[None, :]
    dist = np.stack([kind * TQ + r - c for kind in range(3)]).astype(np.int32)
    return _bias_lookup(rel_bias, dist)


def _sample_bias_rows(rel_bias):
    key_pos = np.arange(S_ALL)[None, :]
    q_pos = PAST + np.arange(T_DEC)[:, None]
    dist = (q_pos - key_pos).astype(np.int32)
    return jnp.transpose(_bias_lookup(rel_bias, dist), (1, 0, 2)).reshape(QROWS, S_ALL)


def kernel(x_prompt, x_sample, p_prompt, p_sample, cache_k, cache_v, cache_kidx, state_ssm_re, state_ssm_im, page_table, ln_emb_g, ln_emb_b, w_in, ssm_a_re, ssm_a_im, ssm_log_dt, ssm_b_re, ssm_b_im, ssm_c_re, ssm_c_im, ssm_d, ssm_w_glu, rel_bias, w_out, ln1_g, ln1_b, ffn_w_gate, ffn_w_up, ffn_w_down, moe_w_router, moe_b_router, moe_w_gate, moe_w_up, moe_w_down, ple_w_gate, ple_w_proj, ln2_g, ln2_b):
    bsz, seq, _ = x_prompt.shape
    dbsz, dseq, _ = x_sample.shape
    assert dseq == T_DEC and page_table.shape[1] == N_PAGES and dbsz % NB == 0
    mp = bsz * seq
    ms = dbsz * dseq
    tm = 512
    tiles_per_seq = seq // tm
    topk_p = min(TOPK_MAX, seq // 4)
    topk_s = min(TOPK_MAX, (PAST + dseq) // 4)
    n_pool = cache_k.shape[1]

    row2 = lambda a: a.reshape(1, -1)
    xp = _ln_call(x_prompt.reshape(mp, D_MODEL), row2(ln_emb_g), row2(ln_emb_b), tm)
    xs = _ln_call(x_sample.reshape(ms, D_MODEL), row2(ln_emb_g), row2(ln_emb_b), tm)

    ckT = jnp.transpose(cache_k, (0, 1, 3, 4, 2)).reshape(DEPTH, n_pool, 128, PAGE_SIZE)
    cvT = jnp.transpose(cache_v, (0, 1, 3, 4, 2)).reshape(DEPTH, n_pool, 128, PAGE_SIZE)
    ckiT = jnp.transpose(cache_kidx, (0, 1, 3, 2))
    bias_p = _prompt_bias_tiles(rel_bias)
    bias_s = _sample_bias_rows(rel_bias)
    h0_prompt = jnp.zeros((bsz, N_STATE), F32)

    w_in_all = jnp.pad(w_in, ((0, 0), (0, 0), (0, D_IN_PAD - D_IN))).astype(BF16)
    a_re, a_im, bblk, cblk = _ssm_matrices(ssm_a_re, ssm_a_im, ssm_log_dt, ssm_b_re, ssm_b_im, ssm_c_re, ssm_c_im)
    d_all = ssm_d.reshape(DEPTH, 1, D_SSM)
    wglu_all = ssm_w_glu.astype(BF16)
    w_out_all = w_out.astype(BF16)
    wpg_all = ple_w_gate.astype(BF16)
    wpp_all = ple_w_proj.astype(BF16)
    ffn_wg, ffn_wu, ffn_wd = ffn_w_gate.astype(BF16), ffn_w_up.astype(BF16), ffn_w_down.astype(BF16)
    moe_wg, moe_wu, moe_wd = moe_w_gate.astype(BF16), moe_w_up.astype(BF16), moe_w_down.astype(BF16)
    wr_pad_all = jnp.pad(moe_w_router, ((0, 0), (0, 0), (0, 128 - N_EXPERTS)))
    br_pad_all = jnp.pad(moe_b_router, ((0, 0), (0, 128 - N_EXPERTS))).reshape(-1, 1, 128)
    pp_all = p_prompt.reshape(DEPTH, mp, D_PLE)
    ps_all = p_sample.reshape(DEPTH, ms, D_PLE)
    s5_params = (a_re, a_im, bblk, cblk, d_all, wglu_all)

    outs = [[] for _ in range(10)]
    for i in range(DEPTH):
        j = i // 2

        def channel_mixer(x1, tm_cm, routed):
            if i % 2 == 0:
                return _ffn_call(x1, ffn_wg, ffn_wu, ffn_wd, j, tm_cm, D_FF // 2)
            if routed:
                return _routed_moe(x1, wr_pad_all[j], br_pad_all[j], moe_wg, moe_wu, moe_wd, j)
            gates = _router_call(x1, wr_pad_all[j], br_pad_all[j], 512)
            return _moe_call(x1, gates, moe_wg, moe_wu, moe_wd, j, tm_cm)

        u_t, qT, k, v, qiT, tail, wT, k2, vT, kib = _inproj_prompt_call(xp, w_in_all, i, tm, bsz, seq)
        ssm_t, hre, him = _s5_call(u_t.reshape(seq * bsz, D_SSM), h0_prompt, h0_prompt, *s5_params, i,
                                   rows_per_step=bsz, steps=128)
        attn = _attn_prompt_call(qiT, wT, kib, qT, k2, vT, bias_p, bsz, seq, topk_p)
        x1 = _outproj_call(xp, ssm_t.reshape(seq, bsz * D_SSM), attn, w_out_all, i, row2(ln1_g[i]), row2(ln1_b[i]),
                           tm, tiles_per_seq)
        cm = channel_mixer(x1, 512, True)
        xp = _ple_ln2_call(x1, cm, pp_all, i, wpg_all, wpp_all, row2(ln2_g[i]), row2(ln2_b[i]), tm)
        outs[0].append(k.reshape(bsz, seq, N_KV_HEADS, HEAD_DIM))
        outs[1].append(v.reshape(bsz, seq, N_KV_HEADS, HEAD_DIM))
        outs[2].append(tail[:, :IDX_DIM].reshape(bsz, seq, IDX_DIM))
        outs[3].append(hre.reshape(bsz, N_SSM_GROUPS, SSM_STATE))
        outs[4].append(him.reshape(bsz, N_SSM_GROUPS, SSM_STATE))

        u, q, k, v, qi, tail, kb, vb, kib = _inproj_call(xs, w_in_all, i, tm)
        u_t = jnp.transpose(u.reshape(dbsz, dseq, D_SSM), (1, 0, 2)).reshape(ms, D_SSM)
        ssm_t, hre, him = _s5_call(u_t, state_ssm_re[i].reshape(dbsz, N_STATE), state_ssm_im[i].reshape(dbsz, N_STATE),
                                   *s5_params, i, rows_per_step=dbsz, steps=dseq)
        ssm = jnp.transpose(ssm_t.reshape(dseq, dbsz, D_SSM), (1, 0, 2)).reshape(ms, D_SSM)
        q4 = q.reshape(dbsz, dseq, N_KV_HEADS, N_REP, HEAD_DIM)
        q_exp = jnp.einsum('btgrd,gh->btgrhd', q4, jnp.eye(N_KV_HEADS, dtype=BF16)).reshape(dbsz, QROWS, 128)
        pad_new = lambda a: jnp.transpose(
            jnp.pad(a.reshape(dbsz, dseq, -1), ((0, 0), (0, PAGE_SIZE - dseq), (0, 0))), (0, 2, 1))
        o_s = _attn_sample_call(page_table, qi.reshape(dbsz, QROWS, IDX_DIM),
                                tail[:, IDX_DIM:IDX_DIM + N_IDX_HEADS].reshape(dbsz, QROWS, 1), q_exp,
                                pad_new(kib), pad_new(kb), pad_new(vb), bias_s,
                                ckT, cvT, ckiT, i, topk_s)
        o5 = o_s.reshape(dbsz, dseq, N_KV_HEADS, N_REP, N_KV_HEADS, HEAD_DIM)
        attn = jnp.einsum('btgrhd,gh->btgrd', o5, jnp.eye(N_KV_HEADS, dtype=F32)).reshape(ms, D_ATTN).astype(BF16)
        x1 = _outproj_call(xs, ssm, attn, w_out_all, i, row2(ln1_g[i]), row2(ln1_b[i]), tm, 0)
        cm = channel_mixer(x1, 512, False)
        xs = _ple_ln2_call(x1, cm, ps_all, i, wpg_all, wpp_all, row2(ln2_g[i]), row2(ln2_b[i]), tm)
        outs[5].append(k.reshape(dbsz, dseq, N_KV_HEADS, HEAD_DIM))
        outs[6].append(v.reshape(dbsz, dseq, N_KV_HEADS, HEAD_DIM))
        outs[7].append(tail[:, :IDX_DIM].reshape(dbsz, dseq, IDX_DIM))
        outs[8].append(hre.reshape(dbsz, N_SSM_GROUPS, SSM_STATE))
        outs[9].append(him.reshape(dbsz, N_SSM_GROUPS, SSM_STATE))

    return (xp.reshape(bsz, seq, D_MODEL), xs.reshape(dbsz, dseq, D_MODEL)) + tuple(jnp.stack(o) for o in outs)
```

```python
import functools
import math

import numpy as np
import jax
import jax.numpy as jnp
from jax import lax
from jax.experimental import pallas as pl
from jax.experimental.pallas import tpu as pltpu

F32 = jnp.float32
BF16 = jnp.bfloat16
I32 = jnp.int32

D_MODEL = 1024
DEPTH = 4
PAGE_SIZE = 128
D_SSM = 512
D_ATTN = 512
SSM_GROUP = 16
N_SSM_GROUPS = 32
SSM_STATE = 64
HEAD_DIM = 64
N_HEADS = 8
N_KV_HEADS = 2
N_REP = 4
N_IDX_HEADS = 8
IDX_DIM = 64
IDX_SCALE = (IDX_DIM ** -0.5) * (N_IDX_HEADS ** -0.5)
TOPK_MAX = 256
N_BUCKETS = 32
MAX_DISTANCE = 128
D_FF = 2816
N_EXPERTS = 8
D_FF_EXPERT = 1408
D_PLE = 256
LN_EPS = 1e-5
ALPHA = (2 * DEPTH) ** 0.25
D_IN = 1864
D_IN_PAD = 1920
N_STATE = N_SSM_GROUPS * SSM_STATE
N_SCHUNK = 4
SCHUNK = N_STATE // N_SCHUNK
UCHUNK = D_SSM // N_SCHUNK

INT_MIN = -2 ** 31
NEG_BIG = -1e30
VMEM_LIMIT = 56 * 1024 * 1024


def _cparams(sem):
    return pltpu.CompilerParams(dimension_semantics=sem, vmem_limit_bytes=VMEM_LIMIT)


def _vmem_spec():
    return pl.BlockSpec(memory_space=pltpu.VMEM)


def _layer_norm(x, g, b):
    mu = jnp.mean(x, axis=-1, keepdims=True)
    xc = x - mu
    var = jnp.mean(xc * xc, axis=-1, keepdims=True)
    return xc * lax.rsqrt(var + LN_EPS) * g + b


def _dot(a, b):
    return jnp.dot(a, b, preferred_element_type=F32)


def _ln_kernel(x_ref, g_ref, b_ref, o_ref):
    o_ref[...] = _layer_norm(x_ref[...], g_ref[...], b_ref[...])


def _ln_call(x, g, b, tm):
    m = x.shape[0]
    return pl.pallas_call(
        _ln_kernel,
        grid=(m // tm,),
        in_specs=[pl.BlockSpec((tm, D_MODEL), lambda i: (i, 0)), _vmem_spec(), _vmem_spec()],
        out_specs=pl.BlockSpec((tm, D_MODEL), lambda i: (i, 0)),
        out_shape=jax.ShapeDtypeStruct((m, D_MODEL), F32),
        compiler_params=_cparams(("parallel",)),
        name="ln_embed",
    )(x, g, b)


def _inproj_kernel(x_ref, w_ref, u_ref, q_ref, k_ref, v_ref, qi_ref, tail_ref, kb_ref, vb_ref, kib_ref):
    xb = x_ref[...].astype(BF16)
    u_ref[...] = _dot(xb, w_ref[:, 0:512])
    q_ref[...] = (_dot(xb, w_ref[:, 512:1024]) * (HEAD_DIM ** -0.5)).astype(BF16)
    k = _dot(xb, w_ref[:, 1024:1152])
    v = _dot(xb, w_ref[:, 1152:1280])
    k_ref[...] = k
    v_ref[...] = v
    kb_ref[...] = k.astype(BF16)
    vb_ref[...] = v.astype(BF16)
    qi_ref[...] = _dot(xb, w_ref[:, 1280:1792]).astype(BF16)
    tail = _dot(xb, w_ref[:, 1792:1920])
    tail_ref[...] = tail
    kib_ref[...] = tail[:, 0:IDX_DIM].astype(BF16)


def _inproj_prompt_kernel(x_ref, w_ref, u_ref, qT_ref, k_ref, v_ref, qiT_ref, tail_ref, wT_ref, k2_ref, vT_ref,
                          kib_ref):
    xb = x_ref[...].astype(BF16)
    u_ref[...] = _dot(xb, w_ref[:, 0:512])
    qT_ref[...] = (_dot(xb, w_ref[:, 512:1024]) * (HEAD_DIM ** -0.5)).T.astype(BF16)
    k = _dot(xb, w_ref[:, 1024:1152])
    v = _dot(xb, w_ref[:, 1152:1280])
    k_ref[...] = k
    v_ref[...] = v
    kb = k.astype(BF16)
    for g in range(N_KV_HEADS):
        k2_ref[g] = kb[:, HEAD_DIM * g:HEAD_DIM * (g + 1)]
    for c in range(vT_ref.shape[0]):
        vT_ref[c] = v[TK * c:TK * (c + 1), :].T.astype(BF16)
    qiT_ref[...] = _dot(xb, w_ref[:, 1280:1792]).T.astype(BF16)
    tail = _dot(xb, w_ref[:, 1792:1920])
    tail_ref[...] = tail
    wT_ref[...] = tail.T[IDX_DIM:IDX_DIM + N_IDX_HEADS, :]
    kib_ref[...] = tail[:, 0:IDX_DIM].astype(BF16)


def _inproj_prompt_call(x, w_all, layer, tm, bsz, seq):
    m = x.shape[0]
    n = seq // tm
    kt = tm // TK
    row = lambda i: (i, 0)
    col = lambda i: (0, i)
    out_shape = (
        jax.ShapeDtypeStruct((seq, bsz * D_SSM), F32),
        jax.ShapeDtypeStruct((512, m), BF16),
        jax.ShapeDtypeStruct((m, 128), F32),
        jax.ShapeDtypeStruct((m, 128), F32),
        jax.ShapeDtypeStruct((512, m), BF16),
        jax.ShapeDtypeStruct((m, 128), F32),
        jax.ShapeDtypeStruct((N_IDX_HEADS, m), F32),
        jax.ShapeDtypeStruct((N_KV_HEADS, m, HEAD_DIM), BF16),
        jax.ShapeDtypeStruct((bsz, seq // TK, 128, TK), BF16),
        jax.ShapeDtypeStruct((m, IDX_DIM), BF16),
    )
    out_specs = (
        pl.BlockSpec((tm, D_SSM), lambda i: (i % n, i // n)),
        pl.BlockSpec((512, tm), col), pl.BlockSpec((tm, 128), row), pl.BlockSpec((tm, 128), row),
        pl.BlockSpec((512, tm), col), pl.BlockSpec((tm, 128), row), pl.BlockSpec((N_IDX_HEADS, tm), col),
        pl.BlockSpec((N_KV_HEADS, tm, HEAD_DIM), lambda i: (0, i, 0)),
        pl.BlockSpec((None, kt, 128, TK), lambda i: (i // n, i % n, 0, 0)),
        pl.BlockSpec((tm, IDX_DIM), row),
    )
    return pl.pallas_call(
        _inproj_prompt_kernel,
        grid=(m // tm,),
        in_specs=[pl.BlockSpec((tm, D_MODEL), row),
                  pl.BlockSpec((None, D_MODEL, D_IN_PAD), lambda i: (layer, 0, 0))],
        out_specs=out_specs,
        out_shape=out_shape,
        compiler_params=_cparams(("parallel",)),
        name="in_proj_prompt",
    )(x, w_all)


def _inproj_call(x, w_all, layer, tm):
    m = x.shape[0]
    row = lambda i: (i, 0)
    out_shape = (
        jax.ShapeDtypeStruct((m, D_SSM), F32),
        jax.ShapeDtypeStruct((m, 512), BF16),
        jax.ShapeDtypeStruct((m, 128), F32),
        jax.ShapeDtypeStruct((m, 128), F32),
        jax.ShapeDtypeStruct((m, 512), BF16),
        jax.ShapeDtypeStruct((m, 128), F32),
        jax.ShapeDtypeStruct((m, 128), BF16),
        jax.ShapeDtypeStruct((m, 128), BF16),
        jax.ShapeDtypeStruct((m, IDX_DIM), BF16),
    )
    out_specs = (
        pl.BlockSpec((tm, D_SSM), row),
        pl.BlockSpec((tm, 512), row), pl.BlockSpec((tm, 128), row), pl.BlockSpec((tm, 128), row),
        pl.BlockSpec((tm, 512), row), pl.BlockSpec((tm, 128), row), pl.BlockSpec((tm, 128), row),
        pl.BlockSpec((tm, 128), row), pl.BlockSpec((tm, IDX_DIM), row),
    )
    return pl.pallas_call(
        _inproj_kernel,
        grid=(m // tm,),
        in_specs=[pl.BlockSpec((tm, D_MODEL), row),
                  pl.BlockSpec((None, D_MODEL, D_IN_PAD), lambda i: (layer, 0, 0))],
        out_specs=out_specs,
        out_shape=out_shape,
        compiler_params=_cparams(("parallel",)),
        name="in_proj",
    )(x, w_all)


def _s5_kernel(u_ref, h0re_ref, h0im_ref, are_ref, aim_ref, bblk_ref, cblk_ref, d_ref, wglu_ref,
               out_ref, hre_ref, him_ref, hs_ref, st_ref, y_ref, *, rows_per_step, steps):
    c = pl.program_id(0)
    rps = rows_per_step
    n_rows = rps * steps

    @pl.when(c == 0)
    def _():
        st_ref[0] = h0re_ref[...]
        st_ref[1] = h0im_ref[...]

    ub = u_ref[...].astype(BF16)
    for j in range(N_SCHUNK):
        hs_ref[j] = _dot(ub[:, UCHUNK * j:UCHUNK * (j + 1)], bblk_ref[j])

    for j in range(N_SCHUNK):
        cols = slice(SCHUNK * j, SCHUNK * (j + 1))
        a_re = jnp.broadcast_to(are_ref[:, cols], (8, SCHUNK))
        a_im = jnp.broadcast_to(aim_ref[:, cols], (8, SCHUNK))
        for rg in range(rps // 8):
            rsl = slice(8 * rg, 8 * (rg + 1))

            def step(t, carry, j=j, rg=rg, a_re=a_re, a_im=a_im):
                h_re, h_im = carry
                row = pl.multiple_of(t * rps + 8 * rg, 8)
                n_re = a_re * h_re - a_im * h_im + hs_ref[j, pl.ds(row, 8), 0:SCHUNK]
                n_im = a_re * h_im + a_im * h_re + hs_ref[j, pl.ds(row, 8), SCHUNK:2 * SCHUNK]
                hs_ref[j, pl.ds(row, 8), 0:SCHUNK] = n_re
                hs_ref[j, pl.ds(row, 8), SCHUNK:2 * SCHUNK] = n_im
                return n_re, n_im

            h_re, h_im = lax.fori_loop(0, steps, step, (st_ref[0, rsl, cols], st_ref[1, rsl, cols]),
                                       unroll=min(steps, 8))
            st_ref[0, rsl, cols] = h_re
            st_ref[1, rsl, cols] = h_im

    rc = 256
    for r0 in range(0, n_rows, rc):
        for j in range(N_SCHUNK):
            y_ref[r0:r0 + rc, UCHUNK * j:UCHUNK * (j + 1)] = _dot(
                hs_ref[j, r0:r0 + rc, :].astype(BF16), cblk_ref[j])
    y = y_ref[...] + d_ref[...] * u_ref[...]
    g = jax.nn.gelu(y).astype(BF16)
    z = _dot(g, wglu_ref[...])
    out_ref[...] = (z[:, :D_SSM] * jax.nn.sigmoid(z[:, D_SSM:])).astype(out_ref.dtype)

    @pl.when(c == pl.num_programs(0) - 1)
    def _():
        hre_ref[...] = st_ref[0]
        him_ref[...] = st_ref[1]


def _s5_call(u_t, h0_re, h0_im, a_re, a_im, bblk, cblk, d, wglu, layer, rows_per_step, steps):
    n_rows_total = u_t.shape[0]
    n_rows = rows_per_step * steps
    kern = functools.partial(_s5_kernel, rows_per_step=rows_per_step, steps=steps)

    def layer_spec(a):
        nd = a.ndim - 1
        return pl.BlockSpec((None,) + a.shape[1:], lambda c: (layer,) + (0,) * nd)

    return pl.pallas_call(
        kern,
        grid=(n_rows_total // n_rows,),
        in_specs=[pl.BlockSpec((n_rows, D_SSM), lambda c: (c, 0)), _vmem_spec(), _vmem_spec()]
        + [layer_spec(a) for a in (a_re, a_im, bblk, cblk, d, wglu)],
        out_specs=(pl.BlockSpec((n_rows, D_SSM), lambda c: (c, 0)),
                   pl.BlockSpec((rows_per_step, N_STATE), lambda c: (0, 0)),
                   pl.BlockSpec((rows_per_step, N_STATE), lambda c: (0, 0))),
        out_shape=(jax.ShapeDtypeStruct((n_rows_total, D_SSM), BF16),
                   jax.ShapeDtypeStruct((rows_per_step, N_STATE), F32),
                   jax.ShapeDtypeStruct((rows_per_step, N_STATE), F32)),
        scratch_shapes=[pltpu.VMEM((N_SCHUNK, n_rows, 2 * SCHUNK), F32),
                        pltpu.VMEM((2, rows_per_step, N_STATE), F32),
                        pltpu.VMEM((n_rows, D_SSM), F32)],
        compiler_params=_cparams(("arbitrary",)),
        name="s5_mixer",
    )(u_t, h0_re, h0_im, a_re, a_im, bblk, cblk, d, wglu)


def _sortable_key(score):
    bits = pltpu.bitcast(score + 0.0, I32)
    return jnp.where(bits < 0, bits ^ 0x7FFFFFFF, bits)


TQ = 256
TK = 256


def _attn_prompt_kernel(qiT_ref, wT_ref, kib_ref, qT_ref, k_ref, vT_ref, bias_ref, o_ref,
                        keys_ref, s_ref, acc_ref, p_ref, *, topk):
    i = pl.program_id(1)
    n_kt = i + 1
    sub = lax.broadcasted_iota(I32, (TK, TQ), 0)
    lane = lax.broadcasted_iota(I32, (TK, TQ), 1)
    causal_diag = sub <= lane

    def score_body(j, carry):
        kt = kib_ref[pl.ds(pl.multiple_of(j * TK, TK), TK), :]
        acc = jnp.zeros((TK, TQ), F32)
        for h in range(N_IDX_HEADS):
            s = _dot(kt, qiT_ref[IDX_DIM * h:IDX_DIM * (h + 1), :])
            acc = acc + jnp.maximum(s, 0.0) * wT_ref[h:h + 1, :]
        key = _sortable_key(acc * IDX_SCALE)
        ok = jnp.logical_or(j < i, causal_diag)
        keys_ref[j] = jnp.where(ok, key, INT_MIN)
        return carry

    lax.fori_loop(0, n_kt, score_body, 0)

    def count(pred_fn):
        def body(j, c):
            hit = pred_fn(j, keys_ref[j]).astype(I32)
            return c + jnp.sum(hit.reshape(TK // 8, 8, TQ), axis=0)
        c8 = lax.fori_loop(0, n_kt, body, jnp.zeros((8, TQ), I32))
        return jnp.sum(c8, axis=0, keepdims=True)

    zero = jnp.zeros((1, TQ), I32)
    thr = jnp.where(count(lambda j, kk: kk >= zero) >= topk, zero, jnp.full((1, TQ), INT_MIN, I32))

    def bit_body(bi, thr):
        cand = thr | jnp.left_shift(jnp.int32(1), 30 - bi)
        return jnp.where(count(lambda j, kk: kk >= cand) >= topk, cand, thr)

    thr = lax.fori_loop(0, 31, bit_body, thr)

    cnt_ge = count(lambda j, kk: kk >= thr)
    p_ref[...] = jnp.full((1, TQ), 2 ** 30, I32)

    @pl.when(jnp.max(cnt_ge) > topk)
    def _():
        need = topk - count(lambda j, kk: kk > thr)

        def pos_body(bi, pos):
            cand = pos | jnp.left_shift(jnp.int32(1), 10 - bi)
            before = count(lambda j, kk: jnp.logical_and(kk == thr, sub + j * TK < cand))
            return jnp.where(before < need, cand, pos)

        p_ref[...] = lax.fori_loop(0, 11, pos_body, zero)

    pos = p_ref[...]
    acc_ref[...] = jnp.zeros(acc_ref.shape, F32)

    def logits_body(j, mx):
        kk = keys_ref[j]
        tie_ok = jnp.logical_and(kk == thr, sub + j * TK <= pos)
        sel = jnp.logical_and(jnp.logical_or(kk > thr, tie_ok), kk != INT_MIN)
        neg = jnp.where(sel, 0.0, NEG_BIG)
        kind = jnp.minimum(i - j, 2)
        row0 = pl.multiple_of(j * TK, TK)
        new = []
        for h in range(N_HEADS):
            g = h // N_REP
            s = _dot(k_ref[g, pl.ds(row0, TK), :], qT_ref[HEAD_DIM * h:HEAD_DIM * (h + 1), :])
            s = s + bias_ref[h, kind] + neg
            s_ref[h, j] = s
            new.append(jnp.maximum(mx[h], jnp.max(s.reshape(TK // 8, 8, TQ), axis=0)))
        return tuple(new)

    mx = lax.fori_loop(0, n_kt, logits_body, tuple(jnp.full((8, TQ), NEG_BIG, F32) for _ in range(N_HEADS)))
    mx = tuple(jnp.broadcast_to(jnp.max(m, axis=0, keepdims=True), (8, TQ)) for m in mx)

    def pv_body(j, ls):
        new = []
        for h in range(N_HEADS):
            g = h // N_REP
            hs = slice(HEAD_DIM * h, HEAD_DIM * (h + 1))
            p = jnp.exp(s_ref[h, j].reshape(TK // 8, 8, TQ) - mx[h][None])
            new.append(ls[h] + jnp.sum(p, axis=0))
            pv = _dot(vT_ref[j, HEAD_DIM * g:HEAD_DIM * (g + 1), :], p.reshape(TK, TQ).astype(BF16))
            acc_ref[hs, :] = acc_ref[hs, :] + pv
        return tuple(new)

    ls = lax.fori_loop(0, n_kt, pv_body, tuple(jnp.zeros((8, TQ), F32) for _ in range(N_HEADS)))

    for h in range(N_HEADS):
        hs = slice(HEAD_DIM * h, HEAD_DIM * (h + 1))
        acc_ref[hs, :] = acc_ref[hs, :] / jnp.sum(ls[h], axis=0, keepdims=True)
    o_ref[...] = acc_ref[...].T.astype(o_ref.dtype)


def _attn_prompt_call(qiT, wT, kib, qT, k2, vT, biasT, bsz, seq, topk):
    nq = seq // TQ
    nkt = seq // TK
    kern = functools.partial(_attn_prompt_kernel, topk=topk)
    return pl.pallas_call(
        kern,
        grid=(bsz, nq),
        in_specs=[
            pl.BlockSpec((512, TQ), lambda b, i: (0, b * nq + i)),
            pl.BlockSpec((N_IDX_HEADS, TQ), lambda b, i: (0, b * nq + i)),
            pl.BlockSpec((seq, IDX_DIM), lambda b, i: (b, 0)),
            pl.BlockSpec((512, TQ), lambda b, i: (0, b * nq + i)),
            pl.BlockSpec((N_KV_HEADS, seq, HEAD_DIM), lambda b, i: (0, b, 0)),
            pl.BlockSpec((None, nkt, 128, TK), lambda b, i: (b, 0, 0, 0)),
            _vmem_spec(),
        ],
        out_specs=pl.BlockSpec((TQ, D_ATTN), lambda b, i: (b * nq + i, 0)),
        out_shape=jax.ShapeDtypeStruct((bsz * seq, D_ATTN), BF16),
        scratch_shapes=[pltpu.VMEM((nkt, TK, TQ), I32),
                        pltpu.VMEM((N_HEADS, nkt, TK, TQ), F32),
                        pltpu.VMEM((D_ATTN, TQ), F32),
                        pltpu.VMEM((1, TQ), I32)],
        compiler_params=_cparams(("parallel", "arbitrary")),
        name="attn_prompt",
    )(qiT, wT, kib, qT, k2, vT, biasT)


NB = 8
N_PAGES = 16
PAST = N_PAGES * PAGE_SIZE
S_ALL = PAST + PAGE_SIZE
T_DEC = 4
QROWS = T_DEC * N_HEADS
_NT = (((1,), (1,)), ((), ()))


def _kidx_copies(pt_ref, ckiT_hbm, kiT_buf, sem, layer, step, slot):
    copies = []
    for n in range(NB):
        for pg in range(N_PAGES):
            page = pt_ref[step * NB + n, pg]
            dst = kiT_buf.at[slot, n, :, pl.ds(pg * PAGE_SIZE, PAGE_SIZE)]
            copies.append(pltpu.make_async_copy(ckiT_hbm.at[layer, page], dst, sem.at[slot]))
    return copies


def _kv_copies(pt_ref, ckT_hbm, cvT_hbm, kT_buf, vT_buf, sem, layer, b, kslot):
    copies = []
    for pg in range(N_PAGES):
        page = pt_ref[b, pg]
        cols = pl.ds(pg * PAGE_SIZE, PAGE_SIZE)
        copies.append(pltpu.make_async_copy(ckT_hbm.at[layer, page], kT_buf.at[kslot, :, cols], sem.at[0, kslot]))
        copies.append(pltpu.make_async_copy(cvT_hbm.at[layer, page], vT_buf.at[kslot, :, cols], sem.at[1, kslot]))
    return copies


def _attn_sample_kernel(pt_ref, qi_ref, w_ref, q_ref, kinT_ref, knT_ref, vnT_ref, bias_ref,
                        ckT_hbm, cvT_hbm, ckiT_hbm, o_ref,
                        kT_buf, vT_buf, kiT_buf, keys_ref, sel_ref, prob_ref, pos_ref, sem_ki, sem_kv, *, layer, topk):
    s = pl.program_id(0)
    n_steps = pl.num_programs(0)
    slot = lax.rem(s, 2)
    b0 = s * NB
    ki_fetch = functools.partial(_kidx_copies, pt_ref, ckiT_hbm, kiT_buf, sem_ki, layer)
    kv_fetch = functools.partial(_kv_copies, pt_ref, ckT_hbm, cvT_hbm, kT_buf, vT_buf, sem_kv, layer)

    @pl.when(s == 0)
    def _():
        for cp in ki_fetch(0, 0):
            cp.start()
        for n in range(NB):
            for cp in kv_fetch(n, n):
                cp.start()

    @pl.when(s + 1 < n_steps)
    def _():
        for cp in ki_fetch(s + 1, 1 - slot):
            cp.start()

    for cp in ki_fetch(s, slot):
        cp.wait()

    lane = lax.broadcasted_iota(I32, (1, S_ALL), 1)

    for n in range(NB):
        qi = qi_ref[n]
        sc = jnp.concatenate([_dot(qi, kiT_buf[slot, n].astype(BF16)), _dot(qi, kinT_ref[n])], axis=1)
        sc = jnp.maximum(sc, 0.0) * w_ref[n]
        for t in range(T_DEC):
            row = jnp.sum(sc[N_IDX_HEADS * t:N_IDX_HEADS * (t + 1), :], axis=0, keepdims=True) * IDX_SCALE
            key = jnp.where(lane <= PAST + t, _sortable_key(row), INT_MIN)
            keys_ref[T_DEC * n + t:T_DEC * n + t + 1, :] = key

    n_grp = NB * T_DEC // 8
    grp = [slice(8 * g, 8 * (g + 1)) for g in range(n_grp)]

    def count(hit):
        return jnp.sum(hit.astype(I32), axis=1, keepdims=True)

    zero = jnp.zeros((8, 1), I32)
    thr = tuple(jnp.where(count(keys_ref[gs, :] >= zero) >= topk, zero, jnp.full((8, 1), INT_MIN, I32))
                for gs in grp)

    def bit_body(bi, thr):
        bit = jnp.left_shift(jnp.int32(1), 30 - bi)
        out = []
        for g, gs in enumerate(grp):
            cand = thr[g] | bit
            out.append(jnp.where(count(keys_ref[gs, :] >= cand) >= topk, cand, thr[g]))
        return tuple(out)

    thr = lax.fori_loop(0, 31, bit_body, thr)
    lane8 = lax.broadcasted_iota(I32, (8, S_ALL), 1)
    most = count(keys_ref[grp[0], :] >= thr[0])
    for g in range(1, n_grp):
        most = jnp.maximum(most, count(keys_ref[grp[g], :] >= thr[g]))
    pos_ref[...] = jnp.full(pos_ref.shape, 2 ** 30, I32)

    @pl.when(jnp.max(most) > topk)
    def _():
        need = tuple(topk - count(keys_ref[gs, :] > thr[g]) for g, gs in enumerate(grp))

        def pos_body(bi, pos):
            bit = jnp.left_shift(jnp.int32(1), 11 - bi)
            out = []
            for g, gs in enumerate(grp):
                cand = pos[g] | bit
                before = count(jnp.logical_and(keys_ref[gs, :] == thr[g], lane8 < cand))
                out.append(jnp.where(before < need[g], cand, pos[g]))
            return tuple(out)

        pos = lax.fori_loop(0, 12, pos_body, tuple(zero for _ in grp))
        for g in range(n_grp):
            pos_ref[g] = pos[g]

    for g, gs in enumerate(grp):
        kk = keys_ref[gs, :]
        tie_ok = jnp.logical_and(kk == thr[g], lane8 <= pos_ref[g])
        sel = jnp.logical_and(jnp.logical_or(kk > thr[g], tie_ok), kk != INT_MIN).astype(I32)
        sel_ref[2 * g] = sel[0:T_DEC]
        sel_ref[2 * g + 1] = sel[T_DEC:2 * T_DEC]

    def attend_body(nn, carry):
        pair = (2 * nn, 2 * nn + 1)
        for n in pair:
            for cp in kv_fetch(b0 + n, n):
                cp.wait()

        for u, n in enumerate(pair):
            q = q_ref[n]
            lg = jnp.concatenate([_dot(q, kT_buf[n].astype(BF16)), _dot(q, knT_ref[n])], axis=1) + bias_ref[...]
            sel_n = sel_ref[n]
            for t in range(T_DEC):
                rs = slice(N_HEADS * t, N_HEADS * (t + 1))
                x = jnp.where(sel_n[t:t + 1, :] > 0, lg[rs, :], NEG_BIG)
                mx = jnp.max(x, axis=1, keepdims=True)
                e = jnp.exp(x - mx)
                prob_ref[u, rs, :] = e / jnp.sum(e, axis=1, keepdims=True)
            p = prob_ref[u].astype(BF16)
            o_ref[n] = (lax.dot_general(p[:, :PAST], vT_buf[n].astype(BF16), _NT, preferred_element_type=F32)
                        + lax.dot_general(p[:, PAST:], vnT_ref[n], _NT, preferred_element_type=F32))

        @pl.when(s + 1 < n_steps)
        def _():
            for n in pair:
                for cp in kv_fetch(b0 + NB + n, n):
                    cp.start()

        return carry

    lax.fori_loop(0, NB // 2, attend_body, 0)


def _attn_sample_call(page_table, qi_s, w_s, q_s, kinT, knT, vnT, bias_s, ckT, cvT, ckiT, layer, topk):
    bsz = qi_s.shape[0]
    kern = functools.partial(_attn_sample_kernel, layer=layer, topk=topk)
    blk3 = lambda shp: pl.BlockSpec((NB,) + shp, lambda s, pt: (s, 0, 0))
    any_spec = pl.BlockSpec(memory_space=pl.ANY)
    grid_spec = pltpu.PrefetchScalarGridSpec(
        num_scalar_prefetch=1,
        grid=(bsz // NB,),
        in_specs=[blk3((QROWS, IDX_DIM)), blk3((QROWS, 1)), blk3((QROWS, 128)),
                  blk3((IDX_DIM, PAGE_SIZE)), blk3((128, PAGE_SIZE)), blk3((128, PAGE_SIZE)),
                  pl.BlockSpec((QROWS, S_ALL), lambda s, pt: (0, 0)),
                  any_spec, any_spec, any_spec],
        out_specs=blk3((QROWS, 128)),
        scratch_shapes=[pltpu.VMEM((NB, 128, PAST), F32),
                        pltpu.VMEM((NB, 128, PAST), F32),
                        pltpu.VMEM((2, NB, IDX_DIM, PAST), F32),
                        pltpu.VMEM((NB * T_DEC, S_ALL), I32),
                        pltpu.VMEM((NB, T_DEC, S_ALL), I32),
                        pltpu.VMEM((2, QROWS, S_ALL), F32),
                        pltpu.VMEM((NB * T_DEC // 8, 8, 1), I32),
                        pltpu.SemaphoreType.DMA((2,)),
                        pltpu.SemaphoreType.DMA((2, NB))],
    )
    return pl.pallas_call(
        kern,
        grid_spec=grid_spec,
        out_shape=jax.ShapeDtypeStruct((bsz, QROWS, 128), F32),
        compiler_params=_cparams(("arbitrary",)),
        name="attn_sample",
    )(page_table, qi_s, w_s, q_s, kinT, knT, vnT, bias_s, ckT, cvT, ckiT)


def _outproj_kernel(x_ref, ssm_ref, attn_ref, wt_ref, wb_ref, g_ref, b_ref, o_ref):
    mixed = _dot(ssm_ref[...], wt_ref[...]) + _dot(attn_ref[...], wb_ref[...])
    o_ref[...] = _layer_norm(ALPHA * x_ref[...] + mixed, g_ref[...], b_ref[...])


def _outproj_call(x, ssm, attn, w_all, layer, g, b, tm, ssm_tmajor_tiles):
    m = x.shape[0]
    row = lambda i: (i, 0)
    if ssm_tmajor_tiles:
        n = ssm_tmajor_tiles
        ssm_spec = pl.BlockSpec((tm, D_SSM), lambda i: (i % n, i // n))
    else:
        ssm_spec = pl.BlockSpec((tm, D_SSM), row)
    return pl.pallas_call(
        _outproj_kernel,
        grid=(m // tm,),
        in_specs=[pl.BlockSpec((tm, D_MODEL), row), ssm_spec, pl.BlockSpec((tm, D_ATTN), row),
                  pl.BlockSpec((None, D_SSM, D_MODEL), lambda i: (layer, 0, 0)),
                  pl.BlockSpec((None, D_ATTN, D_MODEL), lambda i: (layer, 1, 0)),
                  _vmem_spec(), _vmem_spec()],
        out_specs=pl.BlockSpec((tm, D_MODEL), row),
        out_shape=jax.ShapeDtypeStruct((m, D_MODEL), F32),
        compiler_params=_cparams(("parallel",)),
        name="out_proj_ln1",
    )(x, ssm, attn, w_all, w_all, g, b)


def _ffn_kernel(x_ref, wg_ref, wu_ref, wd_ref, o_ref, xb_ref, acc_ref):
    f = pl.program_id(1)

    @pl.when(f == 0)
    def _():
        xb_ref[...] = x_ref[...].astype(BF16)
        acc_ref[...] = jnp.zeros(acc_ref.shape, F32)

    xb = xb_ref[...]
    h = jax.nn.silu(_dot(xb, wg_ref[...])) * _dot(xb, wu_ref[...])
    acc_ref[...] += _dot(h.astype(BF16), wd_ref[...])

    @pl.when(f == pl.num_programs(1) - 1)
    def _():
        o_ref[...] = acc_ref[...]


def _ffn_call(x, wg, wu, wd, j, tm, tf):
    m = x.shape[0]
    dff = wg.shape[2]
    return pl.pallas_call(
        _ffn_kernel,
        grid=(m // tm, dff // tf),
        in_specs=[pl.BlockSpec((tm, D_MODEL), lambda i, f: (i, 0)),
                  pl.BlockSpec((None, D_MODEL, tf), lambda i, f: (j, 0, f)),
                  pl.BlockSpec((None, D_MODEL, tf), lambda i, f: (j, 0, f)),
                  pl.BlockSpec((None, tf, D_MODEL), lambda i, f: (j, f, 0))],
        out_specs=pl.BlockSpec((tm, D_MODEL), lambda i, f: (i, 0)),
        out_shape=jax.ShapeDtypeStruct((m, D_MODEL), F32),
        scratch_shapes=[pltpu.VMEM((tm, D_MODEL), BF16), pltpu.VMEM((tm, D_MODEL), F32)],
        compiler_params=_cparams(("parallel", "arbitrary")),
        name="ffn_swiglu",
    )(x, wg, wu, wd)


def _router_kernel(x_ref, wr_ref, br_ref, g_ref):
    logits = jnp.dot(x_ref[...], wr_ref[...], preferred_element_type=F32,
                     precision=lax.Precision.HIGHEST) + br_ref[...]
    lane = lax.broadcasted_iota(I32, logits.shape, 1)
    logits = jnp.where(lane < N_EXPERTS, logits, -jnp.inf)
    lane_f = lane.astype(F32)
    m1 = jnp.max(logits, axis=1, keepdims=True)
    i1 = jnp.min(jnp.where(logits == m1, lane_f, 128.0), axis=1, keepdims=True)
    rest = jnp.where(lane_f == i1, -jnp.inf, logits)
    m2 = jnp.max(rest, axis=1, keepdims=True)
    i2 = jnp.min(jnp.where(rest == m2, lane_f, 128.0), axis=1, keepdims=True)
    e2 = jnp.exp(m2 - m1)
    den = 1.0 + e2
    g_ref[...] = jnp.where(lane_f == i1, 1.0 / den, 0.0) + jnp.where(lane_f == i2, e2 / den, 0.0)


def _router_call(x, wr_pad, br_pad, tm):
    m = x.shape[0]
    return pl.pallas_call(
        _router_kernel,
        grid=(m // tm,),
        in_specs=[pl.BlockSpec((tm, D_MODEL), lambda i: (i, 0)), _vmem_spec(), _vmem_spec()],
        out_specs=pl.BlockSpec((tm, 128), lambda i: (i, 0)),
        out_shape=jax.ShapeDtypeStruct((m, 128), F32),
        compiler_params=_cparams(("parallel",)),
        name="moe_router",
    )(x, wr_pad, br_pad)


def _moe_kernel(x_ref, gate_ref, wg_ref, wu_ref, wd_ref, o_ref, xb_ref, acc_ref):
    e = pl.program_id(1)

    @pl.when(e == 0)
    def _():
        xb_ref[...] = x_ref[...].astype(BF16)
        acc_ref[...] = jnp.zeros(acc_ref.shape, F32)

    xb = xb_ref[...]
    h = jax.nn.silu(_dot(xb, wg_ref[...])) * _dot(xb, wu_ref[...])
    y = _dot(h.astype(BF16), wd_ref[...])
    gates = gate_ref[...]
    lane = lax.broadcasted_iota(I32, gates.shape, 1)
    ge = jnp.sum(jnp.where(lane == e, gates, 0.0), axis=1, keepdims=True)
    acc_ref[...] += ge * y

    @pl.when(e == pl.num_programs(1) - 1)
    def _():
        o_ref[...] = acc_ref[...]


def _moe_call(x, gates, wg, wu, wd, j, tm):
    m = x.shape[0]
    return pl.pallas_call(
        _moe_kernel,
        grid=(m // tm, N_EXPERTS),
        in_specs=[pl.BlockSpec((tm, D_MODEL), lambda i, e: (i, 0)),
                  pl.BlockSpec((tm, 128), lambda i, e: (i, 0)),
                  pl.BlockSpec((None, None, D_MODEL, D_FF_EXPERT), lambda i, e: (j, e, 0, 0)),
                  pl.BlockSpec((None, None, D_MODEL, D_FF_EXPERT), lambda i, e: (j, e, 0, 0)),
                  pl.BlockSpec((None, None, D_FF_EXPERT, D_MODEL), lambda i, e: (j, e, 0, 0))],
        out_specs=pl.BlockSpec((tm, D_MODEL), lambda i, e: (i, 0)),
        out_shape=jax.ShapeDtypeStruct((m, D_MODEL), F32),
        scratch_shapes=[pltpu.VMEM((tm, D_MODEL), BF16), pltpu.VMEM((tm, D_MODEL), F32)],
        compiler_params=_cparams(("parallel", "arbitrary")),
        name="moe_swiglu",
    )(x, gates, wg, wu, wd)


RT = 256
RT_SMALL = 128
SEG_ALIGN = 16
TG = 512


def _route_kernel(x_ref, wh_ref, wl_ref, br_ref, g_ref, lr_ref, lrT_ref, cnt_ref):
    x = x_ref[...]
    xh = x.astype(BF16)
    xl = (x - xh.astype(F32)).astype(BF16)
    logits = _dot(xh, wh_ref[...]) + (_dot(xh, wl_ref[...]) + _dot(xl, wh_ref[...])) + br_ref[...]
    lane = lax.broadcasted_iota(I32, logits.shape, 1)
    logits = jnp.where(lane < N_EXPERTS, logits, -jnp.inf)
    lane_f = lane.astype(F32)
    m1 = jnp.max(logits, axis=1, keepdims=True)
    i1 = jnp.min(jnp.where(logits == m1, lane_f, 128.0), axis=1, keepdims=True)
    rest = jnp.where(lane_f == i1, -jnp.inf, logits)
    m2 = jnp.max(rest, axis=1, keepdims=True)
    i2 = jnp.min(jnp.where(rest == m2, lane_f, 128.0), axis=1, keepdims=True)
    e2 = jnp.exp(m2 - m1)
    den = 1.0 + e2
    g_ref[...] = jnp.where(lane_f == i1, 1.0 / den, 0.0) + jnp.where(lane_f == i2, e2 / den, 0.0)
    chosen = jnp.logical_or(lane_f == i1, lane_f == i2)
    chosen_f = jnp.where(chosen, 1.0, 0.0)
    r = lax.broadcasted_iota(I32, (RT, RT), 0)
    c = lax.broadcasted_iota(I32, (RT, RT), 1)
    before = jnp.where(r > c, 1.0, 0.0).astype(BF16)
    rank = _dot(before, chosen_f.astype(BF16))
    lr = jnp.where(chosen, rank, -1.0)
    lr_ref[...] = lr
    lrT_ref[...] = lr.T[0:N_EXPERTS, :]
    cnt_ref[...] = jnp.broadcast_to(jnp.sum(chosen_f, axis=0, keepdims=True), (8, 128))


def _route_call(x, wr_pad, br_pad):
    m = x.shape[0]
    nt = m // RT
    wr_hi = wr_pad.astype(BF16)
    wr_lo = (wr_pad - wr_hi.astype(F32)).astype(BF16)
    return pl.pallas_call(
        _route_kernel,
        grid=(nt,),
        in_specs=[pl.BlockSpec((RT, D_MODEL), lambda i: (i, 0)), _vmem_spec(), _vmem_spec(), _vmem_spec()],
        out_specs=(pl.BlockSpec((RT, 128), lambda i: (i, 0)),
                   pl.BlockSpec((RT, 128), lambda i: (i, 0)),
                   pl.BlockSpec((N_EXPERTS, RT), lambda i: (0, i)),
                   pl.BlockSpec((8, 128), lambda i: (i, 0))),
        out_shape=(jax.ShapeDtypeStruct((m, 128), F32),
                   jax.ShapeDtypeStruct((m, 128), F32),
                   jax.ShapeDtypeStruct((N_EXPERTS, m), F32),
                   jax.ShapeDtypeStruct((nt * 8, 128), F32)),
        compiler_params=_cparams(("parallel",)),
        name="moe_route",
    )(x, wr_hi, wr_lo, br_pad)


def _segment_plan(counts, nt, n_ffn_tiles):
    cnt = counts.reshape(nt, 8, 128)[:, 0, :N_EXPERTS].astype(I32)
    r = (cnt + SEG_ALIGN - 1) // SEG_ALIGN * SEG_ALIGN
    used = jnp.sum(r, axis=0)
    region = (used + RT + TG - 1) // TG * TG
    ends = jnp.cumsum(region)
    off = ends - region
    seg = off[None, :] + jnp.cumsum(r, axis=0) - r
    g0 = jnp.arange(n_ffn_tiles, dtype=I32) * TG
    te = jnp.minimum(jnp.sum((g0[:, None] >= ends[None, :]).astype(I32), axis=1), N_EXPERTS - 1)
    valid = jnp.logical_and(g0 < ends[-1], g0 - off[te] < used[te])
    small = (jnp.max(cnt, axis=1) <= RT_SMALL).astype(I32)
    return seg.reshape(-1), small, te, valid.astype(I32)


def _dispatch_copies(seg_ref, buf, xs_hbm, sem, t, slot, rows):
    copies = []
    for e in range(N_EXPERTS):
        start = pl.multiple_of(seg_ref[t * N_EXPERTS + e], SEG_ALIGN)
        copies.append(pltpu.make_async_copy(buf.at[slot, e, pl.ds(0, rows)], xs_hbm.at[pl.ds(start, rows), :],
                                            sem.at[slot]))
    return copies


def _for_block_rows(small_ref, t, fn):
    @pl.when(small_ref[t] > 0)
    def _():
        fn(RT_SMALL)

    @pl.when(small_ref[t] == 0)
    def _():
        fn(RT)


def _dispatch_kernel(seg_ref, small_ref, x_ref, lrT_ref, xs0_hbm, xs_hbm, buf, sem):
    del xs0_hbm
    t = pl.program_id(0)
    slot = lax.rem(t, 2)
    copies = functools.partial(_dispatch_copies, seg_ref, buf, xs_hbm, sem)

    def fill(rows):
        xb = x_ref[...].astype(BF16)
        slot_row = lax.broadcasted_iota(I32, (rows, RT), 0).astype(F32)
        for e in range(N_EXPERTS):
            pick = jnp.where(lrT_ref[e:e + 1, :] == slot_row, 1.0, 0.0).astype(BF16)
            buf[slot, e, 0:rows] = _dot(pick, xb).astype(BF16)

    _for_block_rows(small_ref, t, fill)

    @pl.when(t > 0)
    def _():
        _for_block_rows(small_ref, t - 1, lambda rows: [cp.wait() for cp in copies(t - 1, 1 - slot, rows)])

    _for_block_rows(small_ref, t, lambda rows: [cp.start() for cp in copies(t, slot, rows)])

    @pl.when(t == pl.num_programs(0) - 1)
    def _():
        _for_block_rows(small_ref, t, lambda rows: [cp.wait() for cp in copies(t, slot, rows)])


def _dispatch_call(seg, small, x, lrT, n_slots):
    m = x.shape[0]
    xs0 = jnp.zeros((n_slots, D_MODEL), BF16)
    grid_spec = pltpu.PrefetchScalarGridSpec(
        num_scalar_prefetch=2,
        grid=(m // RT,),
        in_specs=[pl.BlockSpec((RT, D_MODEL), lambda t, seg, small: (t, 0)),
                  pl.BlockSpec((N_EXPERTS, RT), lambda t, seg, small: (0, t)),
                  pl.BlockSpec(memory_space=pl.ANY)],
        out_specs=pl.BlockSpec(memory_space=pl.ANY),
        scratch_shapes=[pltpu.VMEM((2, N_EXPERTS, RT, D_MODEL), BF16), pltpu.SemaphoreType.DMA((2,))],
    )
    return pl.pallas_call(
        _dispatch_kernel,
        grid_spec=grid_spec,
        out_shape=jax.ShapeDtypeStruct((n_slots, D_MODEL), BF16),
        input_output_aliases={4: 0},
        compiler_params=_cparams(("arbitrary",)),
        name="moe_dispatch",
    )(seg, small, x, lrT, xs0)


def _expert_ffn_kernel(te_ref, valid_ref, x_ref, wg_ref, wu_ref, wd_ref, o_ref):
    g = pl.program_id(0)

    @pl.when(valid_ref[g] > 0)
    def _():
        xb = x_ref[...]
        h = jax.nn.silu(_dot(xb, wg_ref[...])) * _dot(xb, wu_ref[...])
        o_ref[...] = _dot(h.astype(BF16), wd_ref[...])

    @pl.when(valid_ref[g] == 0)
    def _():
        o_ref[...] = jnp.zeros(o_ref.shape, F32)


def _expert_ffn_call(te, valid, xs, wg, wu, wd, j):
    n_slots = xs.shape[0]
    wmap = lambda g, te, valid: (j, te[g], 0, 0)
    grid_spec = pltpu.PrefetchScalarGridSpec(
        num_scalar_prefetch=2,
        grid=(n_slots // TG,),
        in_specs=[pl.BlockSpec((TG, D_MODEL), lambda g, te, valid: (g, 0)),
                  pl.BlockSpec((None, None, D_MODEL, D_FF_EXPERT), wmap),
                  pl.BlockSpec((None, None, D_MODEL, D_FF_EXPERT), wmap),
                  pl.BlockSpec((None, None, D_FF_EXPERT, D_MODEL), wmap)],
        out_specs=pl.BlockSpec((TG, D_MODEL), lambda g, te, valid: (g, 0)),
    )
    return pl.pallas_call(
        _expert_ffn_kernel,
        grid_spec=grid_spec,
        out_shape=jax.ShapeDtypeStruct((n_slots, D_MODEL), F32),
        compiler_params=_cparams(("arbitrary",)),
        name="moe_expert_ffn",
    )(te, valid, xs, wg, wu, wd)


def _combine_copies(seg_ref, ys_hbm, buf, sem, t, slot, rows):
    copies = []
    for e in range(N_EXPERTS):
        start = pl.multiple_of(seg_ref[t * N_EXPERTS + e], SEG_ALIGN)
        copies.append(pltpu.make_async_copy(ys_hbm.at[pl.ds(start, rows), :], buf.at[slot, e, pl.ds(0, rows)],
                                            sem.at[slot]))
    return copies


def _combine_kernel(seg_ref, small_ref, gate_ref, lr_ref, ys_hbm, o_ref, buf, sem):
    t = pl.program_id(0)
    slot = lax.rem(t, 2)
    copies = functools.partial(_combine_copies, seg_ref, ys_hbm, buf, sem)

    @pl.when(t == 0)
    def _():
        _for_block_rows(small_ref, 0, lambda rows: [cp.start() for cp in copies(0, 0, rows)])

    @pl.when(t + 1 < pl.num_programs(0))
    def _():
        _for_block_rows(small_ref, t + 1, lambda rows: [cp.start() for cp in copies(t + 1, 1 - slot, rows)])

    def gather(rows):
        for cp in copies(t, slot, rows):
            cp.wait()
        slot_col = lax.broadcasted_iota(I32, (RT, rows), 1).astype(F32)
        acc = jnp.zeros((RT, D_MODEL), F32)
        for e in range(N_EXPERTS):
            pick = jnp.where(lr_ref[:, e:e + 1] == slot_col, 1.0, 0.0).astype(BF16)
            y = buf[slot, e, 0:rows]
            hi = y.astype(BF16)
            lo = (y - hi.astype(F32)).astype(BF16)
            acc = acc + gate_ref[:, e:e + 1] * (_dot(pick, hi) + _dot(pick, lo))
        o_ref[...] = acc

    _for_block_rows(small_ref, t, gather)


def _combine_call(seg, small, gates, lr, ys):
    m = gates.shape[0]
    grid_spec = pltpu.PrefetchScalarGridSpec(
        num_scalar_prefetch=2,
        grid=(m // RT,),
        in_specs=[pl.BlockSpec((RT, 128), lambda t, seg, small: (t, 0)),
                  pl.BlockSpec((RT, 128), lambda t, seg, small: (t, 0)),
                  pl.BlockSpec(memory_space=pl.ANY)],
        out_specs=pl.BlockSpec((RT, D_MODEL), lambda t, seg, small: (t, 0)),
        scratch_shapes=[pltpu.VMEM((2, N_EXPERTS, RT, D_MODEL), F32), pltpu.SemaphoreType.DMA((2,))],
    )
    return pl.pallas_call(
        _combine_kernel,
        grid_spec=grid_spec,
        out_shape=jax.ShapeDtypeStruct((m, D_MODEL), F32),
        compiler_params=_cparams(("arbitrary",)),
        name="moe_combine",
    )(seg, small, gates, lr, ys)


def _routed_moe(x, wr_pad, br_pad, wg, wu, wd, j):
    m = x.shape[0]
    nt = m // RT
    max_rows = 2 * m + nt * N_EXPERTS * (SEG_ALIGN - 1) + N_EXPERTS * (RT + TG - 1)
    n_ffn_tiles = -(-max_rows // TG)
    gates, lr, lrT, counts = _route_call(x, wr_pad, br_pad)
    seg, small, te, valid = _segment_plan(counts, nt, n_ffn_tiles)
    xs = _dispatch_call(seg, small, x, lrT, n_ffn_tiles * TG)
    ys = _expert_ffn_call(te, valid, xs, wg, wu, wd, j)
    return _combine_call(seg, small, gates, lr, ys)


def _ple_ln2_kernel(x_ref, cm_ref, p_ref, wpg_ref, wpp_ref, g_ref, b_ref, o_ref):
    x = x_ref[...]
    gate = jax.nn.sigmoid(_dot(x.astype(BF16), wpg_ref[...]))
    ple = gate * _dot(p_ref[...].astype(BF16), wpp_ref[...])
    o_ref[...] = _layer_norm(ALPHA * x + cm_ref[...] + ple, g_ref[...], b_ref[...])


def _ple_ln2_call(x, cm, p_all, layer, wpg, wpp, g, b, tm):
    m = x.shape[0]
    row = lambda i: (i, 0)
    return pl.pallas_call(
        _ple_ln2_kernel,
        grid=(m // tm,),
        in_specs=[pl.BlockSpec((tm, D_MODEL), row), pl.BlockSpec((tm, D_MODEL), row),
                  pl.BlockSpec((None, tm, D_PLE), lambda i: (layer, i, 0)),
                  pl.BlockSpec((None, D_MODEL, D_MODEL), lambda i: (layer, 0, 0)),
                  pl.BlockSpec((None, D_PLE, D_MODEL), lambda i: (layer, 0, 0)),
                  _vmem_spec(), _vmem_spec()],
        out_specs=pl.BlockSpec((tm, D_MODEL), row),
        out_shape=jax.ShapeDtypeStruct((m, D_MODEL), F32),
        compiler_params=_cparams(("parallel",)),
        name="ple_ln2",
    )(x, cm, p_all, wpg, wpp, g, b)


def _t5_bucket(dist):
    max_exact = N_BUCKETS // 2
    n = jnp.maximum(dist, 0)
    nf = jnp.maximum(n, 1).astype(F32)
    large = max_exact + (jnp.log(nf / max_exact) / math.log(MAX_DISTANCE / max_exact)
                         * (N_BUCKETS - max_exact)).astype(I32)
    large = jnp.minimum(large, N_BUCKETS - 1)
    return jnp.where(n < max_exact, n, large)


def _ssm_matrices(a_re, a_im, log_dt, b_re, b_im, c_re, c_im):
    lam = lax.complex(a_re, a_im)
    dt = jnp.exp(log_dt)[:, :, None]
    a_bar = jnp.exp(lam * dt)
    b_bar = ((a_bar - 1.0) / lam)[..., None] * lax.complex(b_re, b_im)
    gpc = N_SSM_GROUPS // N_SCHUNK
    eye = jnp.eye(gpc, dtype=F32)

    def in_block(x):
        x = x.reshape(DEPTH, N_SCHUNK, gpc, SSM_STATE, SSM_GROUP)
        blk = jnp.einsum('ljgnc,gh->ljgchn', x, eye)
        return blk.reshape(DEPTH, N_SCHUNK, gpc * SSM_GROUP, gpc * SSM_STATE)

    def out_block(x):
        x = x.reshape(DEPTH, N_SCHUNK, gpc, SSM_GROUP, SSM_STATE)
        blk = jnp.einsum('ljgcn,gh->ljgnhc', x, eye)
        return blk.reshape(DEPTH, N_SCHUNK, gpc * SSM_STATE, gpc * SSM_GROUP)

    bblk = jnp.concatenate([in_block(b_bar.real), in_block(b_bar.imag)], axis=3).astype(BF16)
    cblk = jnp.concatenate([out_block(c_re), out_block(-c_im)], axis=2).astype(BF16)
    return (a_bar.real.reshape(DEPTH, 1, N_STATE), a_bar.imag.reshape(DEPTH, 1, N_STATE), bblk, cblk)


def _bias_lookup(rel_bias, dist):
    bucket = _t5_bucket(jnp.asarray(dist))[None]
    out = jnp.zeros((N_HEADS,) + dist.shape, F32)
    for b in range(N_BUCKETS):
        out = jnp.where(bucket == b, rel_bias[b].reshape((N_HEADS,) + (1,) * dist.ndim), out)
    return out


def _prompt_bias_tiles(rel_bias):
    c = np.arange(TK)[:, None]
    r = np.arange(TQ)[None, :]
    dist = np.stack([kind * TQ + r - c for kind in range(3)]).astype(np.int32)
    return _bias_lookup(rel_bias, dist)


def _sample_bias_rows(rel_bias):
    key_pos = np.arange(S_ALL)[None, :]
    q_pos = PAST + np.arange(T_DEC)[:, None]
    dist = (q_pos - key_pos).astype(np.int32)
    return jnp.transpose(_bias_lookup(rel_bias, dist), (1, 0, 2)).reshape(QROWS, S_ALL)


def kernel(x_prompt, x_sample, p_prompt, p_sample, cache_k, cache_v, cache_kidx, state_ssm_re, state_ssm_im, page_table, ln_emb_g, ln_emb_b, w_in, ssm_a_re, ssm_a_im, ssm_log_dt, ssm_b_re, ssm_b_im, ssm_c_re, ssm_c_im, ssm_d, ssm_w_glu, rel_bias, w_out, ln1_g, ln1_b, ffn_w_gate, ffn_w_up, ffn_w_down, moe_w_router, moe_b_router, moe_w_gate, moe_w_up, moe_w_down, ple_w_gate, ple_w_proj, ln2_g, ln2_b):
    bsz, seq, _ = x_prompt.shape
    dbsz, dseq, _ = x_sample.shape
    assert dseq == T_DEC and page_table.shape[1] == N_PAGES and dbsz % NB == 0
    mp = bsz * seq
    ms = dbsz * dseq
    tm = 512
    tiles_per_seq = seq // tm
    topk_p = min(TOPK_MAX, seq // 4)
    topk_s = min(TOPK_MAX, (PAST + dseq) // 4)
    n_pool = cache_k.shape[1]

    row2 = lambda a: a.reshape(1, -1)
    xp = _ln_call(x_prompt.reshape(mp, D_MODEL), row2(ln_emb_g), row2(ln_emb_b), tm)
    xs = _ln_call(x_sample.reshape(ms, D_MODEL), row2(ln_emb_g), row2(ln_emb_b), tm)

    ckT = jnp.transpose(cache_k, (0, 1, 3, 4, 2)).reshape(DEPTH, n_pool, 128, PAGE_SIZE)
    cvT = jnp.transpose(cache_v, (0, 1, 3, 4, 2)).reshape(DEPTH, n_pool, 128, PAGE_SIZE)
    ckiT = jnp.transpose(cache_kidx, (0, 1, 3, 2))
    bias_p = _prompt_bias_tiles(rel_bias)
    bias_s = _sample_bias_rows(rel_bias)
    h0_prompt = jnp.zeros((bsz, N_STATE), F32)

    w_in_all = jnp.pad(w_in, ((0, 0), (0, 0), (0, D_IN_PAD - D_IN))).astype(BF16)
    a_re, a_im, bblk, cblk = _ssm_matrices(ssm_a_re, ssm_a_im, ssm_log_dt, ssm_b_re, ssm_b_im, ssm_c_re, ssm_c_im)
    d_all = ssm_d.reshape(DEPTH, 1, D_SSM)
    wglu_all = ssm_w_glu.astype(BF16)
    w_out_all = w_out.astype(BF16)
    wpg_all = ple_w_gate.astype(BF16)
    wpp_all = ple_w_proj.astype(BF16)
    ffn_wg, ffn_wu, ffn_wd = ffn_w_gate.astype(BF16), ffn_w_up.astype(BF16), ffn_w_down.astype(BF16)
    moe_wg, moe_wu, moe_wd = moe_w_gate.astype(BF16), moe_w_up.astype(BF16), moe_w_down.astype(BF16)
    wr_pad_all = jnp.pad(moe_w_router, ((0, 0), (0, 0), (0, 128 - N_EXPERTS)))
    br_pad_all = jnp.pad(moe_b_router, ((0, 0), (0, 128 - N_EXPERTS))).reshape(-1, 1, 128)
    pp_all = p_prompt.reshape(DEPTH, mp, D_PLE)
    ps_all = p_sample.reshape(DEPTH, ms, D_PLE)
    s5_params = (a_re, a_im, bblk, cblk, d_all, wglu_all)

    outs = [[] for _ in range(10)]
    for i in range(DEPTH):
        j = i // 2

        def channel_mixer(x1, tm_cm, routed):
            if i % 2 == 0:
                return _ffn_call(x1, ffn_wg, ffn_wu, ffn_wd, j, tm_cm, D_FF // 2)
            if routed:
                return _routed_moe(x1, wr_pad_all[j], br_pad_all[j], moe_wg, moe_wu, moe_wd, j)
            gates = _router_call(x1, wr_pad_all[j], br_pad_all[j], 512)
            return _moe_call(x1, gates, moe_wg, moe_wu, moe_wd, j, tm_cm)

        u_t, qT, k, v, qiT, tail, wT, k2, vT, kib = _inproj_prompt_call(xp, w_in_all, i, tm, bsz, seq)
        ssm_t, hre, him = _s5_call(u_t.reshape(seq * bsz, D_SSM), h0_prompt, h0_prompt, *s5_params, i,
                                   rows_per_step=bsz, steps=128)
        attn = _attn_prompt_call(qiT, wT, kib, qT, k2, vT, bias_p, bsz, seq, topk_p)
        x1 = _outproj_call(xp, ssm_t.reshape(seq, bsz * D_SSM), attn, w_out_all, i, row2(ln1_g[i]), row2(ln1_b[i]),
                           tm, tiles_per_seq)
        cm = channel_mixer(x1, 512, True)
        xp = _ple_ln2_call(x1, cm, pp_all, i, wpg_all, wpp_all, row2(ln2_g[i]), row2(ln2_b[i]), tm)
        outs[0].append(k.reshape(bsz, seq, N_KV_HEADS, HEAD_DIM))
        outs[1].append(v.reshape(bsz, seq, N_KV_HEADS, HEAD_DIM))
        outs[2].append(tail[:, :IDX_DIM].reshape(bsz, seq, IDX_DIM))
        outs[3].append(hre.reshape(bsz, N_SSM_GROUPS, SSM_STATE))
        outs[4].append(him.reshape(bsz, N_SSM_GROUPS, SSM_STATE))

        u, q, k, v, qi, tail, kb, vb, kib = _inproj_call(xs, w_in_all, i, tm)
        u_t = jnp.transpose(u.reshape(dbsz, dseq, D_SSM), (1, 0, 2)).reshape(ms, D_SSM)
        ssm_t, hre, him = _s5_call(u_t, state_ssm_re[i].reshape(dbsz, N_STATE), state_ssm_im[i].reshape(dbsz, N_STATE),
                                   *s5_params, i, rows_per_step=dbsz, steps=dseq)
        ssm = jnp.transpose(ssm_t.reshape(dseq, dbsz, D_SSM), (1, 0, 2)).reshape(ms, D_SSM)
        q4 = q.reshape(dbsz, dseq, N_KV_HEADS, N_REP, HEAD_DIM)
        q_exp = jnp.einsum('btgrd,gh->btgrhd', q4, jnp.eye(N_KV_HEADS, dtype=BF16)).reshape(dbsz, QROWS, 128)
        pad_new = lambda a: jnp.transpose(
            jnp.pad(a.reshape(dbsz, dseq, -1), ((0, 0), (0, PAGE_SIZE - dseq), (0, 0))), (0, 2, 1))
        o_s = _attn_sample_call(page_table, qi.reshape(dbsz, QROWS, IDX_DIM),
                                tail[:, IDX_DIM:IDX_DIM + N_IDX_HEADS].reshape(dbsz, QROWS, 1), q_exp,
                                pad_new(kib), pad_new(kb), pad_new(vb), bias_s,
                                ckT, cvT, ckiT, i, topk_s)
        o5 = o_s.reshape(dbsz, dseq, N_KV_HEADS, N_REP, N_KV_HEADS, HEAD_DIM)
        attn = jnp.einsum('btgrhd,gh->btgrd', o5, jnp.eye(N_KV_HEADS, dtype=F32)).reshape(ms, D_ATTN).astype(BF16)
        x1 = _outproj_call(xs, ssm, attn, w_out_all, i, row2(ln1_g[i]), row2(ln1_b[i]), tm, 0)
        cm = channel_mixer(x1, 512, False)
        xs = _ple_ln2_call(x1, cm, ps_all, i, wpg_all, wpp_all, row2(ln2_g[i]), row2(ln2_b[i]), tm)
        outs[5].append(k.reshape(dbsz, dseq, N_KV_HEADS, HEAD_DIM))
        outs[6].append(v.reshape(dbsz, dseq, N_KV_HEADS, HEAD_DIM))
        outs[7].append(tail[:, :IDX_DIM].reshape(dbsz, dseq, IDX_DIM))
        outs[8].append(hre.reshape(dbsz, N_SSM_GROUPS, SSM_STATE))
        outs[9].append(him.reshape(dbsz, N_SSM_GROUPS, SSM_STATE))

    return (xp.reshape(bsz, seq, D_MODEL), xs.reshape(dbsz, dseq, D_MODEL)) + tuple(jnp.stack(o) for o in outs)
```

```python
import functools
import math

import numpy as np
import jax
import jax.numpy as jnp
from jax import lax
from jax.experimental import pallas as pl
from jax.experimental.pallas import tpu as pltpu

F32 = jnp.float32
BF16 = jnp.bfloat16
I32 = jnp.int32

D_MODEL = 1024
DEPTH = 4
PAGE_SIZE = 128
D_SSM = 512
D_ATTN = 512
SSM_GROUP = 16
N_SSM_GROUPS = 32
SSM_STATE = 64
HEAD_DIM = 64
N_HEADS = 8
N_KV_HEADS = 2
N_REP = 4
N_IDX_HEADS = 8
IDX_DIM = 64
IDX_SCALE = (IDX_DIM ** -0.5) * (N_IDX_HEADS ** -0.5)
TOPK_MAX = 256
N_BUCKETS = 32
MAX_DISTANCE = 128
D_FF = 2816
N_EXPERTS = 8
D_FF_EXPERT = 1408
D_PLE = 256
LN_EPS = 1e-5
ALPHA = (2 * DEPTH) ** 0.25
D_IN = 1864
D_IN_PAD = 1920
N_STATE = N_SSM_GROUPS * SSM_STATE
N_SCHUNK = 4
SCHUNK = N_STATE // N_SCHUNK
UCHUNK = D_SSM // N_SCHUNK

INT_MIN = -2 ** 31
NEG_BIG = -1e30
VMEM_LIMIT = 56 * 1024 * 1024


def _cparams(sem):
    return pltpu.CompilerParams(dimension_semantics=sem, vmem_limit_bytes=VMEM_LIMIT)


def _vmem_spec():
    return pl.BlockSpec(memory_space=pltpu.VMEM)


def _layer_norm(x, g, b):
    mu = jnp.mean(x, axis=-1, keepdims=True)
    xc = x - mu
    var = jnp.mean(xc * xc, axis=-1, keepdims=True)
    return xc * lax.rsqrt(var + LN_EPS) * g + b


def _dot(a, b):
    return jnp.dot(a, b, preferred_element_type=F32)


def _ln_kernel(x_ref, g_ref, b_ref, o_ref):
    o_ref[...] = _layer_norm(x_ref[...], g_ref[...], b_ref[...])


def _ln_call(x, g, b, tm):
    m = x.shape[0]
    return pl.pallas_call(
        _ln_kernel,
        grid=(m // tm,),
        in_specs=[pl.BlockSpec((tm, D_MODEL), lambda i: (i, 0)), _vmem_spec(), _vmem_spec()],
        out_specs=pl.BlockSpec((tm, D_MODEL), lambda i: (i, 0)),
        out_shape=jax.ShapeDtypeStruct((m, D_MODEL), F32),
        compiler_params=_cparams(("parallel",)),
        name="ln_embed",
    )(x, g, b)


def _inproj_kernel(x_ref, w_ref, u_ref, q_ref, k_ref, v_ref, qi_ref, tail_ref, kb_ref, vb_ref, kib_ref):
    xb = x_ref[...].astype(BF16)
    u_ref[...] = _dot(xb, w_ref[:, 0:512])
    q_ref[...] = (_dot(xb, w_ref[:, 512:1024]) * (HEAD_DIM ** -0.5)).astype(BF16)
    k = _dot(xb, w_ref[:, 1024:1152])
    v = _dot(xb, w_ref[:, 1152:1280])
    k_ref[...] = k
    v_ref[...] = v
    kb_ref[...] = k.astype(BF16)
    vb_ref[...] = v.astype(BF16)
    qi_ref[...] = _dot(xb, w_ref[:, 1280:1792]).astype(BF16)
    tail = _dot(xb, w_ref[:, 1792:1920])
    tail_ref[...] = tail
    kib_ref[...] = tail[:, 0:IDX_DIM].astype(BF16)


def _inproj_prompt_kernel(x_ref, w_ref, u_ref, qT_ref, k_ref, v_ref, qiT_ref, tail_ref, wT_ref, k2_ref, vT_ref,
                          kib_ref):
    xb = x_ref[...].astype(BF16)
    u_ref[...] = _dot(xb, w_ref[:, 0:512])
    qT_ref[...] = (_dot(xb, w_ref[:, 512:1024]) * (HEAD_DIM ** -0.5)).T.astype(BF16)
    k = _dot(xb, w_ref[:, 1024:1152])
    v = _dot(xb, w_ref[:, 1152:1280])
    k_ref[...] = k
    v_ref[...] = v
    kb = k.astype(BF16)
    for g in range(N_KV_HEADS):
        k2_ref[g] = kb[:, HEAD_DIM * g:HEAD_DIM * (g + 1)]
    for c in range(vT_ref.shape[0]):
        vT_ref[c] = v[TK * c:TK * (c + 1), :].T.astype(BF16)
    qiT_ref[...] = _dot(xb, w_ref[:, 1280:1792]).T.astype(BF16)
    tail = _dot(xb, w_ref[:, 1792:1920])
    tail_ref[...] = tail
    wT_ref[...] = tail.T[IDX_DIM:IDX_DIM + N_IDX_HEADS, :]
    kib_ref[...] = tail[:, 0:IDX_DIM].astype(BF16)


def _inproj_prompt_call(x, w_all, layer, tm, bsz, seq):
    m = x.shape[0]
    n = seq // tm
    kt = tm // TK
    row = lambda i: (i, 0)
    col = lambda i: (0, i)
    out_shape = (
        jax.ShapeDtypeStruct((seq, bsz * D_SSM), F32),
        jax.ShapeDtypeStruct((512, m), BF16),
        jax.ShapeDtypeStruct((m, 128), F32),
        jax.ShapeDtypeStruct((m, 128), F32),
        jax.ShapeDtypeStruct((512, m), BF16),
        jax.ShapeDtypeStruct((m, 128), F32),
        jax.ShapeDtypeStruct((N_IDX_HEADS, m), F32),
        jax.ShapeDtypeStruct((N_KV_HEADS, m, HEAD_DIM), BF16),
        jax.ShapeDtypeStruct((bsz, seq // TK, 128, TK), BF16),
        jax.ShapeDtypeStruct((m, IDX_DIM), BF16),
    )
    out_specs = (
        pl.BlockSpec((tm, D_SSM), lambda i: (i % n, i // n)),
        pl.BlockSpec((512, tm), col), pl.BlockSpec((tm, 128), row), pl.BlockSpec((tm, 128), row),
        pl.BlockSpec((512, tm), col), pl.BlockSpec((tm, 128), row), pl.BlockSpec((N_IDX_HEADS, tm), col),
        pl.BlockSpec((N_KV_HEADS, tm, HEAD_DIM), lambda i: (0, i, 0)),
        pl.BlockSpec((None, kt, 128, TK), lambda i: (i // n, i % n, 0, 0)),
        pl.BlockSpec((tm, IDX_DIM), row),
    )
    return pl.pallas_call(
        _inproj_prompt_kernel,
        grid=(m // tm,),
        in_specs=[pl.BlockSpec((tm, D_MODEL), row),
                  pl.BlockSpec((None, D_MODEL, D_IN_PAD), lambda i: (layer, 0, 0))],
        out_specs=out_specs,
        out_shape=out_shape,
        compiler_params=_cparams(("parallel",)),
        name="in_proj_prompt",
    )(x, w_all)


def _inproj_call(x, w_all, layer, tm):
    m = x.shape[0]
    row = lambda i: (i, 0)
    out_shape = (
        jax.ShapeDtypeStruct((m, D_SSM), F32),
        jax.ShapeDtypeStruct((m, 512), BF16),
        jax.ShapeDtypeStruct((m, 128), F32),
        jax.ShapeDtypeStruct((m, 128), F32),
        jax.ShapeDtypeStruct((m, 512), BF16),
        jax.ShapeDtypeStruct((m, 128), F32),
        jax.ShapeDtypeStruct((m, 128), BF16),
        jax.ShapeDtypeStruct((m, 128), BF16),
        jax.ShapeDtypeStruct((m, IDX_DIM), BF16),
    )
    out_specs = (
        pl.BlockSpec((tm, D_SSM), row),
        pl.BlockSpec((tm, 512), row), pl.BlockSpec((tm, 128), row), pl.BlockSpec((tm, 128), row),
        pl.BlockSpec((tm, 512), row), pl.BlockSpec((tm, 128), row), pl.BlockSpec((tm, 128), row),
        pl.BlockSpec((tm, 128), row), pl.BlockSpec((tm, IDX_DIM), row),
    )
    return pl.pallas_call(
        _inproj_kernel,
        grid=(m // tm,),
        in_specs=[pl.BlockSpec((tm, D_MODEL), row),
                  pl.BlockSpec((None, D_MODEL, D_IN_PAD), lambda i: (layer, 0, 0))],
        out_specs=out_specs,
        out_shape=out_shape,
        compiler_params=_cparams(("parallel",)),
        name="in_proj",
    )(x, w_all)


def _s5_kernel(u_ref, h0re_ref, h0im_ref, are_ref, aim_ref, bblk_ref, cblk_ref, d_ref, wglu_ref,
               out_ref, hre_ref, him_ref, hs_ref, st_ref, y_ref, *, rows_per_step, steps):
    c = pl.program_id(0)
    rps = rows_per_step
    n_rows = rps * steps

    @pl.when(c == 0)
    def _():
        st_ref[0] = h0re_ref[...]
        st_ref[1] = h0im_ref[...]

    ub = u_ref[...].astype(BF16)
    for j in range(N_SCHUNK):
        hs_ref[j] = _dot(ub[:, UCHUNK * j:UCHUNK * (j + 1)], bblk_ref[j])

    for j in range(N_SCHUNK):
        cols = slice(SCHUNK * j, SCHUNK * (j + 1))
        a_re = jnp.broadcast_to(are_ref[:, cols], (8, SCHUNK))
        a_im = jnp.broadcast_to(aim_ref[:, cols], (8, SCHUNK))
        for rg in range(rps // 8):
            rsl = slice(8 * rg, 8 * (rg + 1))

            def advance(rows, h_re, h_im, j=j, a_re=a_re, a_im=a_im):
                n_re = a_re * h_re - a_im * h_im + hs_ref[j, rows, 0:SCHUNK]
                n_im = a_re * h_im + a_im * h_re + hs_ref[j, rows, SCHUNK:2 * SCHUNK]
                hs_ref[j, rows, 0:SCHUNK] = n_re
                hs_ref[j, rows, SCHUNK:2 * SCHUNK] = n_im
                return n_re, n_im

            def step(t, carry, rg=rg, advance=advance):
                return advance(pl.ds(pl.multiple_of(t * rps + 8 * rg, 8), 8), *carry)

            h_re, h_im = lax.fori_loop(0, steps, step, (st_ref[0, rsl, cols], st_ref[1, rsl, cols]),
                                       unroll=min(steps, 8))
            st_ref[0, rsl, cols] = h_re
            st_ref[1, rsl, cols] = h_im

    rc = 256
    for r0 in range(0, n_rows, rc):
        for j in range(N_SCHUNK):
            y_ref[r0:r0 + rc, UCHUNK * j:UCHUNK * (j + 1)] = _dot(
                hs_ref[j, r0:r0 + rc, :].astype(BF16), cblk_ref[j])
    y = y_ref[...] + d_ref[...] * u_ref[...]
    g = jax.nn.gelu(y).astype(BF16)
    z = _dot(g, wglu_ref[...])
    out_ref[...] = (z[:, :D_SSM] * jax.nn.sigmoid(z[:, D_SSM:])).astype(out_ref.dtype)

    @pl.when(c == pl.num_programs(0) - 1)
    def _():
        hre_ref[...] = st_ref[0]
        him_ref[...] = st_ref[1]


def _s5_call(u_t, h0_re, h0_im, a_re, a_im, bblk, cblk, d, wglu, layer, rows_per_step, steps):
    n_rows_total = u_t.shape[0]
    n_rows = rows_per_step * steps
    kern = functools.partial(_s5_kernel, rows_per_step=rows_per_step, steps=steps)

    def layer_spec(a):
        nd = a.ndim - 1
        return pl.BlockSpec((None,) + a.shape[1:], lambda c: (layer,) + (0,) * nd)

    return pl.pallas_call(
        kern,
        grid=(n_rows_total // n_rows,),
        in_specs=[pl.BlockSpec((n_rows, D_SSM), lambda c: (c, 0)), _vmem_spec(), _vmem_spec()]
        + [layer_spec(a) for a in (a_re, a_im, bblk, cblk, d, wglu)],
        out_specs=(pl.BlockSpec((n_rows, D_SSM), lambda c: (c, 0)),
                   pl.BlockSpec((rows_per_step, N_STATE), lambda c: (0, 0)),
                   pl.BlockSpec((rows_per_step, N_STATE), lambda c: (0, 0))),
        out_shape=(jax.ShapeDtypeStruct((n_rows_total, D_SSM), BF16),
                   jax.ShapeDtypeStruct((rows_per_step, N_STATE), F32),
                   jax.ShapeDtypeStruct((rows_per_step, N_STATE), F32)),
        scratch_shapes=[pltpu.VMEM((N_SCHUNK, n_rows, 2 * SCHUNK), F32),
                        pltpu.VMEM((2, rows_per_step, N_STATE), F32),
                        pltpu.VMEM((n_rows, D_SSM), F32)],
        compiler_params=_cparams(("arbitrary",)),
        name="s5_mixer",
    )(u_t, h0_re, h0_im, a_re, a_im, bblk, cblk, d, wglu)


def _sortable_key(score):
    bits = pltpu.bitcast(score + 0.0, I32)
    return jnp.where(bits < 0, bits ^ 0x7FFFFFFF, bits)


TQ = 256
TK = 256


def _attn_prompt_kernel(qiT_ref, wT_ref, kib_ref, qT_ref, k_ref, vT_ref, bias_ref, o_ref,
                        keys_ref, s_ref, acc_ref, p_ref, *, topk):
    i = pl.program_id(1)
    n_kt = i + 1
    sub = lax.broadcasted_iota(I32, (TK, TQ), 0)
    lane = lax.broadcasted_iota(I32, (TK, TQ), 1)
    causal_diag = sub <= lane

    def score_body(j, carry):
        kt = kib_ref[pl.ds(pl.multiple_of(j * TK, TK), TK), :]
        acc = jnp.zeros((TK, TQ), F32)
        for h in range(N_IDX_HEADS):
            s = _dot(kt, qiT_ref[IDX_DIM * h:IDX_DIM * (h + 1), :])
            acc = acc + jnp.maximum(s, 0.0) * wT_ref[h:h + 1, :]
        key = _sortable_key(acc * IDX_SCALE)
        ok = jnp.logical_or(j < i, causal_diag)
        keys_ref[j] = jnp.where(ok, key, INT_MIN)
        return carry

    lax.fori_loop(0, n_kt, score_body, 0)

    def count(pred_fn):
        def body(j, c):
            hit = pred_fn(j, keys_ref[j]).astype(I32)
            return c + jnp.sum(hit.reshape(TK // 8, 8, TQ), axis=0)
        c8 = lax.fori_loop(0, n_kt, body, jnp.zeros((8, TQ), I32))
        return jnp.sum(c8, axis=0, keepdims=True)

    zero = jnp.zeros((1, TQ), I32)
    thr = jnp.where(count(lambda j, kk: kk >= zero) >= topk, zero, jnp.full((1, TQ), INT_MIN, I32))

    def bit_body(bi, thr):
        cand = thr | jnp.left_shift(jnp.int32(1), 30 - bi)
        return jnp.where(count(lambda j, kk: kk >= cand) >= topk, cand, thr)

    thr = lax.fori_loop(0, 31, bit_body, thr)

    cnt_ge = count(lambda j, kk: kk >= thr)
    p_ref[...] = jnp.full((1, TQ), 2 ** 30, I32)

    @pl.when(jnp.max(cnt_ge) > topk)
    def _():
        need = topk - count(lambda j, kk: kk > thr)

        def pos_body(bi, pos):
            cand = pos | jnp.left_shift(jnp.int32(1), 10 - bi)
            before = count(lambda j, kk: jnp.logical_and(kk == thr, sub + j * TK < cand))
            return jnp.where(before < need, cand, pos)

        p_ref[...] = lax.fori_loop(0, 11, pos_body, zero)

    pos = p_ref[...]
    acc_ref[...] = jnp.zeros(acc_ref.shape, F32)

    def logits_body(j, mx):
        kk = keys_ref[j]
        tie_ok = jnp.logical_and(kk == thr, sub + j * TK <= pos)
        sel = jnp.logical_and(jnp.logical_or(kk > thr, tie_ok), kk != INT_MIN)
        neg = jnp.where(sel, 0.0, NEG_BIG)
        kind = jnp.minimum(i - j, 2)
        row0 = pl.multiple_of(j * TK, TK)
        new = []
        for h in range(N_HEADS):
            g = h // N_REP
            s = _dot(k_ref[g, pl.ds(row0, TK), :], qT_ref[HEAD_DIM * h:HEAD_DIM * (h + 1), :])
            s = s + bias_ref[h, kind] + neg
            s_ref[h, j] = s
            new.append(jnp.maximum(mx[h], jnp.max(s.reshape(TK // 8, 8, TQ), axis=0)))
        return tuple(new)

    mx = lax.fori_loop(0, n_kt, logits_body, tuple(jnp.full((8, TQ), NEG_BIG, F32) for _ in range(N_HEADS)))
    mx = tuple(jnp.broadcast_to(jnp.max(m, axis=0, keepdims=True), (8, TQ)) for m in mx)

    def pv_body(j, ls):
        new = []
        for h in range(N_HEADS):
            g = h // N_REP
            hs = slice(HEAD_DIM * h, HEAD_DIM * (h + 1))
            p = jnp.exp(s_ref[h, j].reshape(TK // 8, 8, TQ) - mx[h][None])
            new.append(ls[h] + jnp.sum(p, axis=0))
            pv = _dot(vT_ref[j, HEAD_DIM * g:HEAD_DIM * (g + 1), :], p.reshape(TK, TQ).astype(BF16))
            acc_ref[hs, :] = acc_ref[hs, :] + pv
        return tuple(new)

    ls = lax.fori_loop(0, n_kt, pv_body, tuple(jnp.zeros((8, TQ), F32) for _ in range(N_HEADS)))

    for h in range(N_HEADS):
        hs = slice(HEAD_DIM * h, HEAD_DIM * (h + 1))
        acc_ref[hs, :] = acc_ref[hs, :] / jnp.sum(ls[h], axis=0, keepdims=True)
    o_ref[...] = acc_ref[...].T.astype(o_ref.dtype)


def _attn_prompt_call(qiT, wT, kib, qT, k2, vT, biasT, bsz, seq, topk):
    nq = seq // TQ
    nkt = seq // TK
    kern = functools.partial(_attn_prompt_kernel, topk=topk)
    return pl.pallas_call(
        kern,
        grid=(bsz, nq),
        in_specs=[
            pl.BlockSpec((512, TQ), lambda b, i: (0, b * nq + i)),
            pl.BlockSpec((N_IDX_HEADS, TQ), lambda b, i: (0, b * nq + i)),
            pl.BlockSpec((seq, IDX_DIM), lambda b, i: (b, 0)),
            pl.BlockSpec((512, TQ), lambda b, i: (0, b * nq + i)),
            pl.BlockSpec((N_KV_HEADS, seq, HEAD_DIM), lambda b, i: (0, b, 0)),
            pl.BlockSpec((None, nkt, 128, TK), lambda b, i: (b, 0, 0, 0)),
            _vmem_spec(),
        ],
        out_specs=pl.BlockSpec((TQ, D_ATTN), lambda b, i: (b * nq + i, 0)),
        out_shape=jax.ShapeDtypeStruct((bsz * seq, D_ATTN), BF16),
        scratch_shapes=[pltpu.VMEM((nkt, TK, TQ), I32),
                        pltpu.VMEM((N_HEADS, nkt, TK, TQ), F32),
                        pltpu.VMEM((D_ATTN, TQ), F32),
                        pltpu.VMEM((1, TQ), I32)],
        compiler_params=_cparams(("parallel", "arbitrary")),
        name="attn_prompt",
    )(qiT, wT, kib, qT, k2, vT, biasT)


NB = 8
N_PAGES = 16
PAST = N_PAGES * PAGE_SIZE
S_ALL = PAST + PAGE_SIZE
T_DEC = 4
QROWS = T_DEC * N_HEADS
_NT = (((1,), (1,)), ((), ()))


def _kidx_copies(pt_ref, ckiT_hbm, kiT_buf, sem, layer, step, slot):
    copies = []
    for n in range(NB):
        for pg in range(N_PAGES):
            page = pt_ref[step * NB + n, pg]
            dst = kiT_buf.at[slot, n, :, pl.ds(pg * PAGE_SIZE, PAGE_SIZE)]
            copies.append(pltpu.make_async_copy(ckiT_hbm.at[layer, page], dst, sem.at[slot]))
    return copies


def _kv_copies(pt_ref, ckT_hbm, cvT_hbm, kT_buf, vT_buf, sem, layer, b, kslot):
    copies = []
    for pg in range(N_PAGES):
        page = pt_ref[b, pg]
        cols = pl.ds(pg * PAGE_SIZE, PAGE_SIZE)
        copies.append(pltpu.make_async_copy(ckT_hbm.at[layer, page], kT_buf.at[kslot, :, cols], sem.at[0, kslot]))
        copies.append(pltpu.make_async_copy(cvT_hbm.at[layer, page], vT_buf.at[kslot, :, cols], sem.at[1, kslot]))
    return copies


def _attn_sample_kernel(pt_ref, qi_ref, w_ref, q_ref, kin_ref, kn_ref, vn_ref, bias_ref,
                        ckT_hbm, cvT_hbm, ckiT_hbm, o_ref,
                        kT_buf, vT_buf, kiT_buf, keys_ref, sel_ref, prob_ref, pos_ref, sem_ki, sem_kv, *, layer, topk):
    s = pl.program_id(0)
    n_steps = pl.num_programs(0)
    slot = lax.rem(s, 2)
    b0 = s * NB
    ki_fetch = functools.partial(_kidx_copies, pt_ref, ckiT_hbm, kiT_buf, sem_ki, layer)
    kv_fetch = functools.partial(_kv_copies, pt_ref, ckT_hbm, cvT_hbm, kT_buf, vT_buf, sem_kv, layer)

    @pl.when(s == 0)
    def _():
        for cp in ki_fetch(0, 0):
            cp.start()
        for n in range(NB):
            for cp in kv_fetch(n, n):
                cp.start()

    @pl.when(s + 1 < n_steps)
    def _():
        for cp in ki_fetch(s + 1, 1 - slot):
            cp.start()

    for cp in ki_fetch(s, slot):
        cp.wait()

    lane = lax.broadcasted_iota(I32, (1, S_ALL), 1)

    for n in range(NB):
        qi = qi_ref[n]
        sc = jnp.concatenate([_dot(qi, kiT_buf[slot, n].astype(BF16)),
                              lax.dot_general(qi, kin_ref[n], _NT, preferred_element_type=F32)], axis=1)
        sc = jnp.maximum(sc, 0.0) * w_ref[n]
        for t in range(T_DEC):
            row = jnp.sum(sc[N_IDX_HEADS * t:N_IDX_HEADS * (t + 1), :], axis=0, keepdims=True) * IDX_SCALE
            key = jnp.where(lane <= PAST + t, _sortable_key(row), INT_MIN)
            keys_ref[T_DEC * n + t:T_DEC * n + t + 1, :] = key

    n_grp = NB * T_DEC // 8
    grp = [slice(8 * g, 8 * (g + 1)) for g in range(n_grp)]

    def count(hit):
        return jnp.sum(hit.astype(I32), axis=1, keepdims=True)

    zero = jnp.zeros((8, 1), I32)
    thr = tuple(jnp.where(count(keys_ref[gs, :] >= zero) >= topk, zero, jnp.full((8, 1), INT_MIN, I32))
                for gs in grp)

    def bit_body(bi, thr):
        bit = jnp.left_shift(jnp.int32(1), 30 - bi)
        out = []
        for g, gs in enumerate(grp):
            cand = thr[g] | bit
            out.append(jnp.where(count(keys_ref[gs, :] >= cand) >= topk, cand, thr[g]))
        return tuple(out)

    thr = lax.fori_loop(0, 31, bit_body, thr)
    lane8 = lax.broadcasted_iota(I32, (8, S_ALL), 1)
    most = count(keys_ref[grp[0], :] >= thr[0])
    for g in range(1, n_grp):
        most = jnp.maximum(most, count(keys_ref[grp[g], :] >= thr[g]))
    pos_ref[...] = jnp.full(pos_ref.shape, 2 ** 30, I32)

    @pl.when(jnp.max(most) > topk)
    def _():
        need = tuple(topk - count(keys_ref[gs, :] > thr[g]) for g, gs in enumerate(grp))

        def pos_body(bi, pos):
            bit = jnp.left_shift(jnp.int32(1), 11 - bi)
            out = []
            for g, gs in enumerate(grp):
                cand = pos[g] | bit
                before = count(jnp.logical_and(keys_ref[gs, :] == thr[g], lane8 < cand))
                out.append(jnp.where(before < need[g], cand, pos[g]))
            return tuple(out)

        pos = lax.fori_loop(0, 12, pos_body, tuple(zero for _ in grp))
        for g in range(n_grp):
            pos_ref[g] = pos[g]

    for g, gs in enumerate(grp):
        kk = keys_ref[gs, :]
        tie_ok = jnp.logical_and(kk == thr[g], lane8 <= pos_ref[g])
        sel = jnp.logical_and(jnp.logical_or(kk > thr[g], tie_ok), kk != INT_MIN).astype(I32)
        sel_ref[2 * g] = sel[0:T_DEC]
        sel_ref[2 * g + 1] = sel[T_DEC:2 * T_DEC]

    def attend_body(nn, carry):
        pair = (2 * nn, 2 * nn + 1)
        for n in pair:
            for cp in kv_fetch(b0 + n, n):
                cp.wait()

        for u, n in enumerate(pair):
            q = q_ref[n]
            lg = jnp.concatenate([_dot(q, kT_buf[n].astype(BF16)),
                                  lax.dot_general(q, kn_ref[n], _NT, preferred_element_type=F32)], axis=1)
            lg = lg + bias_ref[...]
            sel_n = sel_ref[n]
            for t in range(T_DEC):
                rs = slice(N_HEADS * t, N_HEADS * (t + 1))
                x = jnp.where(sel_n[t:t + 1, :] > 0, lg[rs, :], NEG_BIG)
                mx = jnp.max(x, axis=1, keepdims=True)
                e = jnp.exp(x - mx)
                prob_ref[u, rs, :] = e / jnp.sum(e, axis=1, keepdims=True)
            p = prob_ref[u].astype(BF16)
            o_ref[n] = (lax.dot_general(p[:, :PAST], vT_buf[n].astype(BF16), _NT, preferred_element_type=F32)
                        + _dot(p[:, PAST:], vn_ref[n]))

        @pl.when(s + 1 < n_steps)
        def _():
            for n in pair:
                for cp in kv_fetch(b0 + NB + n, n):
                    cp.start()

        return carry

    lax.fori_loop(0, NB // 2, attend_body, 0)


def _attn_sample_call(page_table, qi_s, w_s, q_s, kin, kn, vn, bias_s, ckT, cvT, ckiT, layer, topk):
    bsz = qi_s.shape[0]
    kern = functools.partial(_attn_sample_kernel, layer=layer, topk=topk)
    blk3 = lambda shp: pl.BlockSpec((NB,) + shp, lambda s, pt: (s, 0, 0))
    any_spec = pl.BlockSpec(memory_space=pl.ANY)
    grid_spec = pltpu.PrefetchScalarGridSpec(
        num_scalar_prefetch=1,
        grid=(bsz // NB,),
        in_specs=[blk3((QROWS, IDX_DIM)), blk3((QROWS, 1)), blk3((QROWS, 128)),
                  blk3((PAGE_SIZE, IDX_DIM)), blk3((PAGE_SIZE, 128)), blk3((PAGE_SIZE, 128)),
                  pl.BlockSpec((QROWS, S_ALL), lambda s, pt: (0, 0)),
                  any_spec, any_spec, any_spec],
        out_specs=blk3((QROWS, 128)),
        scratch_shapes=[pltpu.VMEM((NB, 128, PAST), F32),
                        pltpu.VMEM((NB, 128, PAST), F32),
                        pltpu.VMEM((2, NB, IDX_DIM, PAST), F32),
                        pltpu.VMEM((NB * T_DEC, S_ALL), I32),
                        pltpu.VMEM((NB, T_DEC, S_ALL), I32),
                        pltpu.VMEM((2, QROWS, S_ALL), F32),
                        pltpu.VMEM((NB * T_DEC // 8, 8, 1), I32),
                        pltpu.SemaphoreType.DMA((2,)),
                        pltpu.SemaphoreType.DMA((2, NB))],
    )
    return pl.pallas_call(
        kern,
        grid_spec=grid_spec,
        out_shape=jax.ShapeDtypeStruct((bsz, QROWS, 128), F32),
        compiler_params=_cparams(("arbitrary",)),
        name="attn_sample",
    )(page_table, qi_s, w_s, q_s, kin, kn, vn, bias_s, ckT, cvT, ckiT)


def _outproj_kernel(x_ref, ssm_ref, attn_ref, wt_ref, wb_ref, g_ref, b_ref, o_ref):
    mixed = _dot(ssm_ref[...], wt_ref[...]) + _dot(attn_ref[...], wb_ref[...])
    o_ref[...] = _layer_norm(ALPHA * x_ref[...] + mixed, g_ref[...], b_ref[...])


def _outproj_call(x, ssm, attn, w_all, layer, g, b, tm, ssm_tmajor_tiles):
    m = x.shape[0]
    row = lambda i: (i, 0)
    if ssm_tmajor_tiles:
        n = ssm_tmajor_tiles
        ssm_spec = pl.BlockSpec((tm, D_SSM), lambda i: (i % n, i // n))
    else:
        ssm_spec = pl.BlockSpec((tm, D_SSM), row)
    return pl.pallas_call(
        _outproj_kernel,
        grid=(m // tm,),
        in_specs=[pl.BlockSpec((tm, D_MODEL), row), ssm_spec, pl.BlockSpec((tm, D_ATTN), row),
                  pl.BlockSpec((None, D_SSM, D_MODEL), lambda i: (layer, 0, 0)),
                  pl.BlockSpec((None, D_ATTN, D_MODEL), lambda i: (layer, 1, 0)),
                  _vmem_spec(), _vmem_spec()],
        out_specs=pl.BlockSpec((tm, D_MODEL), row),
        out_shape=jax.ShapeDtypeStruct((m, D_MODEL), F32),
        compiler_params=_cparams(("parallel",)),
        name="out_proj_ln1",
    )(x, ssm, attn, w_all, w_all, g, b)


def _ffn_kernel(x_ref, wg_ref, wu_ref, wd_ref, o_ref, xb_ref, acc_ref):
    f = pl.program_id(1)

    @pl.when(f == 0)
    def _():
        xb_ref[...] = x_ref[...].astype(BF16)
        acc_ref[...] = jnp.zeros(acc_ref.shape, F32)

    xb = xb_ref[...]
    h = jax.nn.silu(_dot(xb, wg_ref[...])) * _dot(xb, wu_ref[...])
    acc_ref[...] += _dot(h.astype(BF16), wd_ref[...])

    @pl.when(f == pl.num_programs(1) - 1)
    def _():
        o_ref[...] = acc_ref[...]


def _ffn_call(x, wg, wu, wd, j, tm, tf):
    m = x.shape[0]
    dff = wg.shape[2]
    return pl.pallas_call(
        _ffn_kernel,
        grid=(m // tm, dff // tf),
        in_specs=[pl.BlockSpec((tm, D_MODEL), lambda i, f: (i, 0)),
                  pl.BlockSpec((None, D_MODEL, tf), lambda i, f: (j, 0, f)),
                  pl.BlockSpec((None, D_MODEL, tf), lambda i, f: (j, 0, f)),
                  pl.BlockSpec((None, tf, D_MODEL), lambda i, f: (j, f, 0))],
        out_specs=pl.BlockSpec((tm, D_MODEL), lambda i, f: (i, 0)),
        out_shape=jax.ShapeDtypeStruct((m, D_MODEL), F32),
        scratch_shapes=[pltpu.VMEM((tm, D_MODEL), BF16), pltpu.VMEM((tm, D_MODEL), F32)],
        compiler_params=_cparams(("parallel", "arbitrary")),
        name="ffn_swiglu",
    )(x, wg, wu, wd)


def _router_kernel(x_ref, wr_ref, br_ref, g_ref):
    logits = jnp.dot(x_ref[...], wr_ref[...], preferred_element_type=F32,
                     precision=lax.Precision.HIGHEST) + br_ref[...]
    lane = lax.broadcasted_iota(I32, logits.shape, 1)
    logits = jnp.where(lane < N_EXPERTS, logits, -jnp.inf)
    lane_f = lane.astype(F32)
    m1 = jnp.max(logits, axis=1, keepdims=True)
    i1 = jnp.min(jnp.where(logits == m1, lane_f, 128.0), axis=1, keepdims=True)
    rest = jnp.where(lane_f == i1, -jnp.inf, logits)
    m2 = jnp.max(rest, axis=1, keepdims=True)
    i2 = jnp.min(jnp.where(rest == m2, lane_f, 128.0), axis=1, keepdims=True)
    e2 = jnp.exp(m2 - m1)
    den = 1.0 + e2
    g_ref[...] = jnp.where(lane_f == i1, 1.0 / den, 0.0) + jnp.where(lane_f == i2, e2 / den, 0.0)


def _router_call(x, wr_pad, br_pad, tm):
    m = x.shape[0]
    return pl.pallas_call(
        _router_kernel,
        grid=(m // tm,),
        in_specs=[pl.BlockSpec((tm, D_MODEL), lambda i: (i, 0)), _vmem_spec(), _vmem_spec()],
        out_specs=pl.BlockSpec((tm, 128), lambda i: (i, 0)),
        out_shape=jax.ShapeDtypeStruct((m, 128), F32),
        compiler_params=_cparams(("parallel",)),
        name="moe_router",
    )(x, wr_pad, br_pad)


def _moe_kernel(x_ref, gate_ref, wg_ref, wu_ref, wd_ref, o_ref, xb_ref, acc_ref):
    e = pl.program_id(1)

    @pl.when(e == 0)
    def _():
        xb_ref[...] = x_ref[...].astype(BF16)
        acc_ref[...] = jnp.zeros(acc_ref.shape, F32)

    xb = xb_ref[...]
    h = jax.nn.silu(_dot(xb, wg_ref[...])) * _dot(xb, wu_ref[...])
    y = _dot(h.astype(BF16), wd_ref[...])
    gates = gate_ref[...]
    lane = lax.broadcasted_iota(I32, gates.shape, 1)
    ge = jnp.sum(jnp.where(lane == e, gates, 0.0), axis=1, keepdims=True)
    acc_ref[...] += ge * y

    @pl.when(e == pl.num_programs(1) - 1)
    def _():
        o_ref[...] = acc_ref[...]


def _moe_call(x, gates, wg, wu, wd, j, tm):
    m = x.shape[0]
    return pl.pallas_call(
        _moe_kernel,
        grid=(m // tm, N_EXPERTS),
        in_specs=[pl.BlockSpec((tm, D_MODEL), lambda i, e: (i, 0)),
                  pl.BlockSpec((tm, 128), lambda i, e: (i, 0)),
                  pl.BlockSpec((None, None, D_MODEL, D_FF_EXPERT), lambda i, e: (j, e, 0, 0)),
                  pl.BlockSpec((None, None, D_MODEL, D_FF_EXPERT), lambda i, e: (j, e, 0, 0)),
                  pl.BlockSpec((None, None, D_FF_EXPERT, D_MODEL), lambda i, e: (j, e, 0, 0))],
        out_specs=pl.BlockSpec((tm, D_MODEL), lambda i, e: (i, 0)),
        out_shape=jax.ShapeDtypeStruct((m, D_MODEL), F32),
        scratch_shapes=[pltpu.VMEM((tm, D_MODEL), BF16), pltpu.VMEM((tm, D_MODEL), F32)],
        compiler_params=_cparams(("parallel", "arbitrary")),
        name="moe_swiglu",
    )(x, gates, wg, wu, wd)


RT = 256
RT_SMALL = 128
SEG_ALIGN = 16
TG = 512


def _route_kernel(x_ref, wh_ref, wl_ref, br_ref, g_ref, lr_ref, lrT_ref, cnt_ref):
    x = x_ref[...]
    xh = x.astype(BF16)
    xl = (x - xh.astype(F32)).astype(BF16)
    logits = _dot(xh, wh_ref[...]) + (_dot(xh, wl_ref[...]) + _dot(xl, wh_ref[...])) + br_ref[...]
    lane = lax.broadcasted_iota(I32, logits.shape, 1)
    logits = jnp.where(lane < N_EXPERTS, logits, -jnp.inf)
    lane_f = lane.astype(F32)
    m1 = jnp.max(logits, axis=1, keepdims=True)
    i1 = jnp.min(jnp.where(logits == m1, lane_f, 128.0), axis=1, keepdims=True)
    rest = jnp.where(lane_f == i1, -jnp.inf, logits)
    m2 = jnp.max(rest, axis=1, keepdims=True)
    i2 = jnp.min(jnp.where(rest == m2, lane_f, 128.0), axis=1, keepdims=True)
    e2 = jnp.exp(m2 - m1)
    den = 1.0 + e2
    g_ref[...] = jnp.where(lane_f == i1, 1.0 / den, 0.0) + jnp.where(lane_f == i2, e2 / den, 0.0)
    chosen = jnp.logical_or(lane_f == i1, lane_f == i2)
    chosen_f = jnp.where(chosen, 1.0, 0.0)
    r = lax.broadcasted_iota(I32, (RT, RT), 0)
    c = lax.broadcasted_iota(I32, (RT, RT), 1)
    before = jnp.where(r > c, 1.0, 0.0).astype(BF16)
    rank = _dot(before, chosen_f.astype(BF16))
    lr = jnp.where(chosen, rank, -1.0)
    lr_ref[...] = lr
    lrT_ref[...] = lr.T[0:N_EXPERTS, :]
    cnt_ref[...] = jnp.broadcast_to(jnp.sum(chosen_f, axis=0, keepdims=True), (8, 128))


def _route_call(x, wr_pad, br_pad):
    m = x.shape[0]
    nt = m // RT
    wr_hi = wr_pad.astype(BF16)
    wr_lo = (wr_pad - wr_hi.astype(F32)).astype(BF16)
    return pl.pallas_call(
        _route_kernel,
        grid=(nt,),
        in_specs=[pl.BlockSpec((RT, D_MODEL), lambda i: (i, 0)), _vmem_spec(), _vmem_spec(), _vmem_spec()],
        out_specs=(pl.BlockSpec((RT, 128), lambda i: (i, 0)),
                   pl.BlockSpec((RT, 128), lambda i: (i, 0)),
                   pl.BlockSpec((N_EXPERTS, RT), lambda i: (0, i)),
                   pl.BlockSpec((8, 128), lambda i: (i, 0))),
        out_shape=(jax.ShapeDtypeStruct((m, 128), F32),
                   jax.ShapeDtypeStruct((m, 128), F32),
                   jax.ShapeDtypeStruct((N_EXPERTS, m), F32),
                   jax.ShapeDtypeStruct((nt * 8, 128), F32)),
        compiler_params=_cparams(("parallel",)),
        name="moe_route",
    )(x, wr_hi, wr_lo, br_pad)


def _segment_plan(counts, nt, n_ffn_tiles):
    cnt = counts.reshape(nt, 8, 128)[:, 0, :N_EXPERTS].astype(I32)
    r = (cnt + SEG_ALIGN - 1) // SEG_ALIGN * SEG_ALIGN
    used = jnp.sum(r, axis=0)
    region = (used + RT + TG - 1) // TG * TG
    ends = jnp.cumsum(region)
    off = ends - region
    seg = off[None, :] + jnp.cumsum(r, axis=0) - r
    g0 = jnp.arange(n_ffn_tiles, dtype=I32) * TG
    te = jnp.minimum(jnp.sum((g0[:, None] >= ends[None, :]).astype(I32), axis=1), N_EXPERTS - 1)
    valid = jnp.logical_and(g0 < ends[-1], g0 - off[te] < used[te])
    small = (jnp.max(cnt, axis=1) <= RT_SMALL).astype(I32)
    return seg.reshape(-1), small, te, valid.astype(I32)


def _dispatch_copies(seg_ref, buf, xs_hbm, sem, t, slot, rows):
    copies = []
    for e in range(N_EXPERTS):
        start = pl.multiple_of(seg_ref[t * N_EXPERTS + e], SEG_ALIGN)
        copies.append(pltpu.make_async_copy(buf.at[slot, e, pl.ds(0, rows)], xs_hbm.at[pl.ds(start, rows), :],
                                            sem.at[slot]))
    return copies


def _for_block_rows(small_ref, t, fn):
    @pl.when(small_ref[t] > 0)
    def _():
        fn(RT_SMALL)

    @pl.when(small_ref[t] == 0)
    def _():
        fn(RT)


def _dispatch_kernel(seg_ref, small_ref, x_ref, lrT_ref, xs0_hbm, xs_hbm, buf, sem):
    del xs0_hbm
    t = pl.program_id(0)
    slot = lax.rem(t, 2)
    copies = functools.partial(_dispatch_copies, seg_ref, buf, xs_hbm, sem)

    def fill(rows):
        xb = x_ref[...].astype(BF16)
        slot_row = lax.broadcasted_iota(I32, (rows, RT), 0).astype(F32)
        for e in range(N_EXPERTS):
            pick = jnp.where(lrT_ref[e:e + 1, :] == slot_row, 1.0, 0.0).astype(BF16)
            buf[slot, e, 0:rows] = _dot(pick, xb).astype(BF16)

    _for_block_rows(small_ref, t, fill)

    @pl.when(t > 0)
    def _():
        _for_block_rows(small_ref, t - 1, lambda rows: [cp.wait() for cp in copies(t - 1, 1 - slot, rows)])

    _for_block_rows(small_ref, t, lambda rows: [cp.start() for cp in copies(t, slot, rows)])

    @pl.when(t == pl.num_programs(0) - 1)
    def _():
        _for_block_rows(small_ref, t, lambda rows: [cp.wait() for cp in copies(t, slot, rows)])


def _dispatch_call(seg, small, x, lrT, n_slots):
    m = x.shape[0]
    xs0 = jnp.zeros((n_slots, D_MODEL), BF16)
    grid_spec = pltpu.PrefetchScalarGridSpec(
        num_scalar_prefetch=2,
        grid=(m // RT,),
        in_specs=[pl.BlockSpec((RT, D_MODEL), lambda t, seg, small: (t, 0)),
                  pl.BlockSpec((N_EXPERTS, RT), lambda t, seg, small: (0, t)),
                  pl.BlockSpec(memory_space=pl.ANY)],
        out_specs=pl.BlockSpec(memory_space=pl.ANY),
        scratch_shapes=[pltpu.VMEM((2, N_EXPERTS, RT, D_MODEL), BF16), pltpu.SemaphoreType.DMA((2,))],
    )
    return pl.pallas_call(
        _dispatch_kernel,
        grid_spec=grid_spec,
        out_shape=jax.ShapeDtypeStruct((n_slots, D_MODEL), BF16),
        input_output_aliases={4: 0},
        compiler_params=_cparams(("arbitrary",)),
        name="moe_dispatch",
    )(seg, small, x, lrT, xs0)


def _expert_ffn_kernel(te_ref, valid_ref, x_ref, wg_ref, wu_ref, wd_ref, o_ref):
    g = pl.program_id(0)

    @pl.when(valid_ref[g] > 0)
    def _():
        xb = x_ref[...]
        h = jax.nn.silu(_dot(xb, wg_ref[...])) * _dot(xb, wu_ref[...])
        o_ref[...] = _dot(h.astype(BF16), wd_ref[...])

    @pl.when(valid_ref[g] == 0)
    def _():
        o_ref[...] = jnp.zeros(o_ref.shape, F32)


def _expert_ffn_call(te, valid, xs, wg, wu, wd, j):
    n_slots = xs.shape[0]
    wmap = lambda g, te, valid: (j, te[g], 0, 0)
    grid_spec = pltpu.PrefetchScalarGridSpec(
        num_scalar_prefetch=2,
        grid=(n_slots // TG,),
        in_specs=[pl.BlockSpec((TG, D_MODEL), lambda g, te, valid: (g, 0)),
                  pl.BlockSpec((None, None, D_MODEL, D_FF_EXPERT), wmap),
                  pl.BlockSpec((None, None, D_MODEL, D_FF_EXPERT), wmap),
                  pl.BlockSpec((None, None, D_FF_EXPERT, D_MODEL), wmap)],
        out_specs=pl.BlockSpec((TG, D_MODEL), lambda g, te, valid: (g, 0)),
    )
    return pl.pallas_call(
        _expert_ffn_kernel,
        grid_spec=grid_spec,
        out_shape=jax.ShapeDtypeStruct((n_slots, D_MODEL), F32),
        compiler_params=_cparams(("arbitrary",)),
        name="moe_expert_ffn",
    )(te, valid, xs, wg, wu, wd)


def _combine_copies(seg_ref, ys_hbm, buf, sem, t, slot, rows):
    copies = []
    for e in range(N_EXPERTS):
        start = pl.multiple_of(seg_ref[t * N_EXPERTS + e], SEG_ALIGN)
        copies.append(pltpu.make_async_copy(ys_hbm.at[pl.ds(start, rows), :], buf.at[slot, e, pl.ds(0, rows)],
                                            sem.at[slot]))
    return copies


def _combine_kernel(seg_ref, small_ref, gate_ref, lr_ref, ys_hbm, o_ref, buf, sem):
    t = pl.program_id(0)
    slot = lax.rem(t, 2)
    copies = functools.partial(_combine_copies, seg_ref, ys_hbm, buf, sem)

    @pl.when(t == 0)
    def _():
        _for_block_rows(small_ref, 0, lambda rows: [cp.start() for cp in copies(0, 0, rows)])

    @pl.when(t + 1 < pl.num_programs(0))
    def _():
        _for_block_rows(small_ref, t + 1, lambda rows: [cp.start() for cp in copies(t + 1, 1 - slot, rows)])

    def gather(rows):
        for cp in copies(t, slot, rows):
            cp.wait()
        slot_col = lax.broadcasted_iota(I32, (RT, rows), 1).astype(F32)
        acc = jnp.zeros((RT, D_MODEL), F32)
        for e in range(N_EXPERTS):
            pick = jnp.where(lr_ref[:, e:e + 1] == slot_col, 1.0, 0.0).astype(BF16)
            y = buf[slot, e, 0:rows]
            hi = y.astype(BF16)
            lo = (y - hi.astype(F32)).astype(BF16)
            acc = acc + gate_ref[:, e:e + 1] * (_dot(pick, hi) + _dot(pick, lo))
        o_ref[...] = acc

    _for_block_rows(small_ref, t, gather)


def _combine_call(seg, small, gates, lr, ys):
    m = gates.shape[0]
    grid_spec = pltpu.PrefetchScalarGridSpec(
        num_scalar_prefetch=2,
        grid=(m // RT,),
        in_specs=[pl.BlockSpec((RT, 128), lambda t, seg, small: (t, 0)),
                  pl.BlockSpec((RT, 128), lambda t, seg, small: (t, 0)),
                  pl.BlockSpec(memory_space=pl.ANY)],
        out_specs=pl.BlockSpec((RT, D_MODEL), lambda t, seg, small: (t, 0)),
        scratch_shapes=[pltpu.VMEM((2, N_EXPERTS, RT, D_MODEL), F32), pltpu.SemaphoreType.DMA((2,))],
    )
    return pl.pallas_call(
        _combine_kernel,
        grid_spec=grid_spec,
        out_shape=jax.ShapeDtypeStruct((m, D_MODEL), F32),
        compiler_params=_cparams(("arbitrary",)),
        name="moe_combine",
    )(seg, small, gates, lr, ys)


def _routed_moe(x, wr_pad, br_pad, wg, wu, wd, j):
    m = x.shape[0]
    nt = m // RT
    max_rows = 2 * m + nt * N_EXPERTS * (SEG_ALIGN - 1) + N_EXPERTS * (RT + TG - 1)
    n_ffn_tiles = -(-max_rows // TG)
    gates, lr, lrT, counts = _route_call(x, wr_pad, br_pad)
    seg, small, te, valid = _segment_plan(counts, nt, n_ffn_tiles)
    xs = _dispatch_call(seg, small, x, lrT, n_ffn_tiles * TG)
    ys = _expert_ffn_call(te, valid, xs, wg, wu, wd, j)
    return _combine_call(seg, small, gates, lr, ys)


def _ple_ln2_kernel(x_ref, cm_ref, p_ref, wpg_ref, wpp_ref, g_ref, b_ref, o_ref):
    x = x_ref[...]
    gate = jax.nn.sigmoid(_dot(x.astype(BF16), wpg_ref[...]))
    ple = gate * _dot(p_ref[...].astype(BF16), wpp_ref[...])
    o_ref[...] = _layer_norm(ALPHA * x + cm_ref[...] + ple, g_ref[...], b_ref[...])


def _ple_ln2_call(x, cm, p_all, layer, wpg, wpp, g, b, tm):
    m = x.shape[0]
    row = lambda i: (i, 0)
    return pl.pallas_call(
        _ple_ln2_kernel,
        grid=(m // tm,),
        in_specs=[pl.BlockSpec((tm, D_MODEL), row), pl.BlockSpec((tm, D_MODEL), row),
                  pl.BlockSpec((None, tm, D_PLE), lambda i: (layer, i, 0)),
                  pl.BlockSpec((None, D_MODEL, D_MODEL), lambda i: (layer, 0, 0)),
                  pl.BlockSpec((None, D_PLE, D_MODEL), lambda i: (layer, 0, 0)),
                  _vmem_spec(), _vmem_spec()],
        out_specs=pl.BlockSpec((tm, D_MODEL), row),
        out_shape=jax.ShapeDtypeStruct((m, D_MODEL), F32),
        compiler_params=_cparams(("parallel",)),
        name="ple_ln2",
    )(x, cm, p_all, wpg, wpp, g, b)


def _t5_bucket(dist):
    max_exact = N_BUCKETS // 2
    n = jnp.maximum(dist, 0)
    nf = jnp.maximum(n, 1).astype(F32)
    large = max_exact + (jnp.log(nf / max_exact) / math.log(MAX_DISTANCE / max_exact)
                         * (N_BUCKETS - max_exact)).astype(I32)
    large = jnp.minimum(large, N_BUCKETS - 1)
    return jnp.where(n < max_exact, n, large)


def _ssm_matrices(a_re, a_im, log_dt, b_re, b_im, c_re, c_im):
    lam = lax.complex(a_re, a_im)
    dt = jnp.exp(log_dt)[:, :, None]
    a_bar = jnp.exp(lam * dt)
    b_bar = ((a_bar - 1.0) / lam)[..., None] * lax.complex(b_re, b_im)
    gpc = N_SSM_GROUPS // N_SCHUNK
    eye = jnp.eye(gpc, dtype=F32)

    def in_block(x):
        x = x.reshape(DEPTH, N_SCHUNK, gpc, SSM_STATE, SSM_GROUP)
        blk = jnp.einsum('ljgnc,gh->ljgchn', x, eye)
        return blk.reshape(DEPTH, N_SCHUNK, gpc * SSM_GROUP, gpc * SSM_STATE)

    def out_block(x):
        x = x.reshape(DEPTH, N_SCHUNK, gpc, SSM_GROUP, SSM_STATE)
        blk = jnp.einsum('ljgcn,gh->ljgnhc', x, eye)
        return blk.reshape(DEPTH, N_SCHUNK, gpc * SSM_STATE, gpc * SSM_GROUP)

    bblk = jnp.concatenate([in_block(b_bar.real), in_block(b_bar.imag)], axis=3).astype(BF16)
    cblk = jnp.concatenate([out_block(c_re), out_block(-c_im)], axis=2).astype(BF16)
    return (a_bar.real.reshape(DEPTH, 1, N_STATE), a_bar.imag.reshape(DEPTH, 1, N_STATE), bblk, cblk)


def _bias_lookup(rel_bias, dist):
    bucket = _t5_bucket(jnp.asarray(dist))[None]
    out = jnp.zeros((N_HEADS,) + dist.shape, F32)
    for b in range(N_BUCKETS):
        out = jnp.where(bucket == b, rel_bias[b].reshape((N_HEADS,) + (1,) * dist.ndim), out)
    return out


def _prompt_bias_tiles(rel_bias):
    c = np.arange(TK)[:, None]
    r = np.arange(TQ)[None, :]
    dist = np.stack([kind * TQ + r - c for kind in range(3)]).astype(np.int32)
    return _bias_lookup(rel_bias, dist)


def _sample_bias_rows(rel_bias):
    key_pos = np.arange(S_ALL)[None, :]
    q_pos = PAST + np.arange(T_DEC)[:, None]
    dist = (q_pos - key_pos).astype(np.int32)
    return jnp.transpose(_bias_lookup(rel_bias, dist), (1, 0, 2)).reshape(QROWS, S_ALL)


def kernel(x_prompt, x_sample, p_prompt, p_sample, cache_k, cache_v, cache_kidx, state_ssm_re, state_ssm_im, page_table, ln_emb_g, ln_emb_b, w_in, ssm_a_re, ssm_a_im, ssm_log_dt, ssm_b_re, ssm_b_im, ssm_c_re, ssm_c_im, ssm_d, ssm_w_glu, rel_bias, w_out, ln1_g, ln1_b, ffn_w_gate, ffn_w_up, ffn_w_down, moe_w_router, moe_b_router, moe_w_gate, moe_w_up, moe_w_down, ple_w_gate, ple_w_proj, ln2_g, ln2_b):
    bsz, seq, _ = x_prompt.shape
    dbsz, dseq, _ = x_sample.shape
    assert dseq == T_DEC and page_table.shape[1] == N_PAGES and dbsz % NB == 0
    mp = bsz * seq
    ms = dbsz * dseq
    tm = 512
    tiles_per_seq = seq // tm
    topk_p = min(TOPK_MAX, seq // 4)
    topk_s = min(TOPK_MAX, (PAST + dseq) // 4)
    n_pool = cache_k.shape[1]

    row2 = lambda a: a.reshape(1, -1)
    xp = _ln_call(x_prompt.reshape(mp, D_MODEL), row2(ln_emb_g), row2(ln_emb_b), tm)
    xs = _ln_call(x_sample.reshape(ms, D_MODEL), row2(ln_emb_g), row2(ln_emb_b), tm)

    ckT = jnp.transpose(cache_k, (0, 1, 3, 4, 2)).reshape(DEPTH, n_pool, 128, PAGE_SIZE)
    cvT = jnp.transpose(cache_v, (0, 1, 3, 4, 2)).reshape(DEPTH, n_pool, 128, PAGE_SIZE)
    ckiT = jnp.transpose(cache_kidx, (0, 1, 3, 2))
    bias_p = _prompt_bias_tiles(rel_bias)
    bias_s = _sample_bias_rows(rel_bias)
    h0_prompt = jnp.zeros((bsz, N_STATE), F32)

    w_in_all = jnp.pad(w_in, ((0, 0), (0, 0), (0, D_IN_PAD - D_IN))).astype(BF16)
    a_re, a_im, bblk, cblk = _ssm_matrices(ssm_a_re, ssm_a_im, ssm_log_dt, ssm_b_re, ssm_b_im, ssm_c_re, ssm_c_im)
    d_all = ssm_d.reshape(DEPTH, 1, D_SSM)
    wglu_all = ssm_w_glu.astype(BF16)
    w_out_all = w_out.astype(BF16)
    wpg_all = ple_w_gate.astype(BF16)
    wpp_all = ple_w_proj.astype(BF16)
    ffn_wg, ffn_wu, ffn_wd = ffn_w_gate.astype(BF16), ffn_w_up.astype(BF16), ffn_w_down.astype(BF16)
    moe_wg, moe_wu, moe_wd = moe_w_gate.astype(BF16), moe_w_up.astype(BF16), moe_w_down.astype(BF16)
    wr_pad_all = jnp.pad(moe_w_router, ((0, 0), (0, 0), (0, 128 - N_EXPERTS)))
    br_pad_all = jnp.pad(moe_b_router, ((0, 0), (0, 128 - N_EXPERTS))).reshape(-1, 1, 128)
    pp_all = p_prompt.reshape(DEPTH, mp, D_PLE)
    ps_all = p_sample.reshape(DEPTH, ms, D_PLE)
    s5_params = (a_re, a_im, bblk, cblk, d_all, wglu_all)

    outs = [[] for _ in range(10)]
    for i in range(DEPTH):
        j = i // 2

        def channel_mixer(x1, tm_cm, routed):
            if i % 2 == 0:
                return _ffn_call(x1, ffn_wg, ffn_wu, ffn_wd, j, tm_cm, D_FF // 2)
            if routed:
                return _routed_moe(x1, wr_pad_all[j], br_pad_all[j], moe_wg, moe_wu, moe_wd, j)
            gates = _router_call(x1, wr_pad_all[j], br_pad_all[j], 512)
            return _moe_call(x1, gates, moe_wg, moe_wu, moe_wd, j, tm_cm)

        u_t, qT, k, v, qiT, tail, wT, k2, vT, kib = _inproj_prompt_call(xp, w_in_all, i, tm, bsz, seq)
        ssm_t, hre, him = _s5_call(u_t.reshape(seq * bsz, D_SSM), h0_prompt, h0_prompt, *s5_params, i,
                                   rows_per_step=bsz, steps=128)
        attn = _attn_prompt_call(qiT, wT, kib, qT, k2, vT, bias_p, bsz, seq, topk_p)
        x1 = _outproj_call(xp, ssm_t.reshape(seq, bsz * D_SSM), attn, w_out_all, i, row2(ln1_g[i]), row2(ln1_b[i]),
                           tm, tiles_per_seq)
        cm = channel_mixer(x1, 512, True)
        xp = _ple_ln2_call(x1, cm, pp_all, i, wpg_all, wpp_all, row2(ln2_g[i]), row2(ln2_b[i]), tm)
        outs[0].append(k.reshape(bsz, seq, N_KV_HEADS, HEAD_DIM))
        outs[1].append(v.reshape(bsz, seq, N_KV_HEADS, HEAD_DIM))
        outs[2].append(tail[:, :IDX_DIM].reshape(bsz, seq, IDX_DIM))
        outs[3].append(hre.reshape(bsz, N_SSM_GROUPS, SSM_STATE))
        outs[4].append(him.reshape(bsz, N_SSM_GROUPS, SSM_STATE))

        u, q, k, v, qi, tail, kb, vb, kib = _inproj_call(xs, w_in_all, i, tm)
        u_t = jnp.transpose(u.reshape(dbsz, dseq, D_SSM), (1, 0, 2)).reshape(ms, D_SSM)
        ssm_t, hre, him = _s5_call(u_t, state_ssm_re[i].reshape(dbsz, N_STATE), state_ssm_im[i].reshape(dbsz, N_STATE),
                                   *s5_params, i, rows_per_step=dbsz, steps=dseq)
        ssm = jnp.transpose(ssm_t.reshape(dseq, dbsz, D_SSM), (1, 0, 2)).reshape(ms, D_SSM)
        q4 = q.reshape(dbsz, dseq, N_KV_HEADS, N_REP, HEAD_DIM)
        q_exp = jnp.einsum('btgrd,gh->btgrhd', q4, jnp.eye(N_KV_HEADS, dtype=BF16)).reshape(dbsz, QROWS, 128)
        pad_new = lambda a: jnp.pad(a.reshape(dbsz, dseq, -1), ((0, 0), (0, PAGE_SIZE - dseq), (0, 0)))
        o_s = _attn_sample_call(page_table, qi.reshape(dbsz, QROWS, IDX_DIM),
                                tail[:, IDX_DIM:IDX_DIM + N_IDX_HEADS].reshape(dbsz, QROWS, 1), q_exp,
                                pad_new(kib), pad_new(kb), pad_new(vb), bias_s,
                                ckT, cvT, ckiT, i, topk_s)
        o5 = o_s.reshape(dbsz, dseq, N_KV_HEADS, N_REP, N_KV_HEADS, HEAD_DIM)
        attn = jnp.einsum('btgrhd,gh->btgrd', o5, jnp.eye(N_KV_HEADS, dtype=F32)).reshape(ms, D_ATTN).astype(BF16)
        x1 = _outproj_call(xs, ssm, attn, w_out_all, i, row2(ln1_g[i]), row2(ln1_b[i]), tm, 0)
        cm = channel_mixer(x1, 512, False)
        xs = _ple_ln2_call(x1, cm, ps_all, i, wpg_all, wpp_all, row2(ln2_g[i]), row2(ln2_b[i]), tm)
        outs[5].append(k.reshape(dbsz, dseq, N_KV_HEADS, HEAD_DIM))
        outs[6].append(v.reshape(dbsz, dseq, N_KV_HEADS, HEAD_DIM))
        outs[7].append(tail[:, :IDX_DIM].reshape(dbsz, dseq, IDX_DIM))
        outs[8].append(hre.reshape(dbsz, N_SSM_GROUPS, SSM_STATE))
        outs[9].append(him.reshape(dbsz, N_SSM_GROUPS, SSM_STATE))

    return (xp.reshape(bsz, seq, D_MODEL), xs.reshape(dbsz, dseq, D_MODEL)) + tuple(jnp.stack(o) for o in outs)
```
